```python
import jax, jax.numpy as jnp
from jax import lax
import numpy as np

D_MODEL = 1024
BATCH = 8
SEQ = 2048
DEPTH = 4

CHUNK = 64
N_MIXERS = 2
CONV_WIDTH = 31
POOL_WINDOWS = (2, 4, 8, 16)
N_POOL_GROUPS = len(POOL_WINDOWS)
GROUP_WIDTH = D_MODEL // N_POOL_GROUPS
D_FF = 4 * D_MODEL
N_CONV_LAYERS = (DEPTH + 1) // 2
N_POOL_LAYERS = DEPTH // 2
EPS = 1e-6

kernel_name = "hybrid_conv_pool_streaming_encoder"


def rmsnorm(x, g):
    xf = x.astype(jnp.float32)
    y = xf * lax.rsqrt(jnp.mean(xf * xf, axis=-1, keepdims=True) + EPS)
    return (y * g.astype(jnp.float32)).astype(x.dtype)


def layernorm(x, g, b):
    xf = x.astype(jnp.float32)
    mu = jnp.mean(xf, axis=-1, keepdims=True)
    var = jnp.mean(jnp.square(xf - mu), axis=-1, keepdims=True)
    y = (xf - mu) * lax.rsqrt(var + EPS)
    return (y * g.astype(jnp.float32) + b.astype(jnp.float32)).astype(x.dtype)


def conformer_conv(h, w1, b1, w_dw, b_dw, ln_g, ln_b, w2, b2):
    u = jnp.einsum('bsd,de->bse', h, w1) + b1
    a, gate = jnp.split(u, 2, axis=-1)
    u = a * jax.nn.sigmoid(gate)
    u = lax.conv_general_dilated(
        u, w_dw[:, None, :].astype(u.dtype),
        window_strides=(1,), padding=[(CONV_WIDTH - 1, 0)],
        dimension_numbers=('NWC', 'WIO', 'NWC'),
        feature_group_count=D_MODEL) + b_dw
    u = jax.nn.silu(layernorm(u, ln_g, ln_b))
    return jnp.einsum('bsd,de->bse', u, w2) + b2


def pool_mixer(h, w_grp, b_grp, scale):
    S = h.shape[1]
    hf = h.astype(jnp.float32)
    cs = jnp.concatenate([jnp.zeros_like(hf[:, :1]), jnp.cumsum(hf, axis=1)], axis=1)
    pos = jnp.arange(S)
    outs = []
    for g, w in enumerate(POOL_WINDOWS):
        sl = slice(g * GROUP_WIDTH, (g + 1) * GROUP_WIDTH)
        lo = jnp.maximum(pos + 1 - w, 0)
        cnt = (pos + 1 - lo).astype(jnp.float32)[None, :, None]
        mean = (cs[:, pos + 1, sl] - cs[:, lo, sl]) / cnt
        outs.append(mean - hf[..., sl])
    p = jnp.concatenate(outs, axis=-1).astype(h.dtype)
    p = p.reshape(p.shape[0], S, N_POOL_GROUPS, GROUP_WIDTH)
    y = jnp.einsum('bsgc,gcd->bsgd', p, w_grp) + b_grp
    return y.reshape(h.shape) * scale


def sqrelu_mlp(h, w1, w2):
    u = jax.nn.relu(jnp.einsum('bsd,df->bsf', h, w1))
    return jnp.einsum('bsf,fd->bsd', u * u, w2)


def setup_inputs(seed: int = 0) -> dict:
    key = jax.random.key(seed)
    ks = jax.random.split(key, 20)
    f = jnp.float32
    D, F, NC, NP, G, GW = D_MODEL, D_FF, N_CONV_LAYERS, N_POOL_LAYERS, N_POOL_GROUPS, GROUP_WIDTH
    nrm = lambda k, s, sc: jax.random.normal(k, s, f) * sc
    return {
        "x": jax.random.normal(ks[0], (BATCH, SEQ, D), f),
        "norm_mix": 1.0 + nrm(ks[1], (DEPTH, D), 0.05),
        "norm_mlp": 1.0 + nrm(ks[2], (DEPTH, D), 0.05),
        "conv_w1": nrm(ks[3], (NC, D, 2 * D), D ** -0.5),
        "conv_b1": nrm(ks[4], (NC, 2 * D), 0.02),
        "conv_dw": nrm(ks[5], (NC, CONV_WIDTH, D), CONV_WIDTH ** -0.5),
        "conv_dwb": nrm(ks[6], (NC, D), 0.02),
        "conv_ln_g": 1.0 + nrm(ks[7], (NC, D), 0.05),
        "conv_ln_b": nrm(ks[8], (NC, D), 0.02),
        "conv_w2": nrm(ks[9], (NC, D, D), D ** -0.5),
        "conv_b2": nrm(ks[10], (NC, D), 0.02),
        "pool_w": nrm(ks[11], (NP, G, GW, GW), GW ** -0.5),
        "pool_b": nrm(ks[12], (NP, G, GW), 0.02),
        "pool_scale": 1.0 + nrm(ks[13], (NP, D), 0.1),
        "mlp_w1": nrm(ks[14], (DEPTH, D, F), D ** -0.5),
        "mlp_w2": nrm(ks[15], (DEPTH, F, D), F ** -0.5),
        "final_norm": 1.0 + nrm(ks[16], (D,), 0.05),
    }


def reference(x, norm_mix, norm_mlp, conv_w1, conv_b1, conv_dw, conv_dwb,
              conv_ln_g, conv_ln_b, conv_w2, conv_b2, pool_w, pool_b,
              pool_scale, mlp_w1, mlp_w2, final_norm):
    h = x
    ic, ip = 0, 0
    for i in range(DEPTH):
        hn = rmsnorm(h, norm_mix[i])
        if i % N_MIXERS == 0:
            mix = conformer_conv(hn, conv_w1[ic], conv_b1[ic], conv_dw[ic], conv_dwb[ic],
                                 conv_ln_g[ic], conv_ln_b[ic], conv_w2[ic], conv_b2[ic])
            ic += 1
        else:
            mix = pool_mixer(hn, pool_w[ip], pool_b[ip], pool_scale[ip])
            ip += 1
        h = h + mix
        h = h + sqrelu_mlp(rmsnorm(h, norm_mlp[i]), mlp_w1[i], mlp_w2[i])
    return rmsnorm(h, final_norm)
```

```python
import functools

import jax
import jax.numpy as jnp
from jax import lax
from jax.experimental import pallas as pl
from jax.experimental.pallas import tpu as pltpu

EPS = 1e-6
CONV_WIDTH = 31
POOL_WINDOWS = (2, 4, 8, 16)

LANES = 128
SUBLANES = 8
CONV_HALO = 32
POOL_HALO = 16
CONV_ROWS = 64
VMEM_LIMIT = 56 * 1024 * 1024

_F32 = jnp.float32
_BF16 = jnp.bfloat16


def _rmsnorm(x, g):
    ms = jnp.mean(x * x, axis=-1, keepdims=True)
    return x * lax.rsqrt(ms + EPS) * g


def _layernorm(x, g, b):
    mu = jnp.mean(x, axis=-1, keepdims=True)
    xc = x - mu
    var = jnp.mean(xc * xc, axis=-1, keepdims=True)
    return xc * lax.rsqrt(var + EPS) * g + b


def _dot(a, b):
    return jnp.dot(a, b, preferred_element_type=_F32)


def _conv_kernel(h_ref, g_ref, w1_ref, b1_ref, wdw_ref, bdw_ref, lng_ref, lnb_ref,
                 w2_ref, b2_ref, o_ref, ubuf, cbuf, *, tile, d):
    s = pl.program_id(1)
    n_slab = d // LANES
    h = h_ref[0]
    hn = _rmsnorm(h, g_ref[...])
    u = _dot(hn.astype(_BF16), w1_ref[...]) + b1_ref[...]
    glu = u[:, :d] * jax.nn.sigmoid(u[:, d:])

    @pl.when(s == 0)
    def _():
        ubuf[:, 0:CONV_HALO, :] = jnp.zeros((n_slab, CONV_HALO, LANES), _F32)

    @pl.when(s > 0)
    def _():
        ubuf[:, 0:CONV_HALO, :] = ubuf[:, tile:tile + CONV_HALO, :]

    for c in range(n_slab):
        ubuf[c, CONV_HALO:CONV_HALO + tile, :] = glu[:, c * LANES:(c + 1) * LANES]

    base = CONV_HALO - (CONV_WIDTH - 1)

    def chunk(i, carry):
        r0 = pl.multiple_of(i * CONV_ROWS, CONV_ROWS)
        for c in range(n_slab):
            lanes = slice(c * LANES, (c + 1) * LANES)
            acc = jnp.broadcast_to(bdw_ref[:, lanes], (CONV_ROWS, LANES))
            for k in range(CONV_WIDTH):
                acc = acc + wdw_ref[k:k + 1, lanes] * ubuf[c, pl.ds(r0 + (base + k), CONV_ROWS), :]
            cbuf[pl.ds(r0, CONV_ROWS), lanes] = acc
        return carry

    lax.fori_loop(0, tile // CONV_ROWS, chunk, 0)

    v = _layernorm(cbuf[...], lng_ref[...], lnb_ref[...])
    v = v * jax.nn.sigmoid(v)
    o_ref[0] = h + _dot(v.astype(_BF16), w2_ref[...]) + b2_ref[...]


def _const_spec(shape):
    return pl.BlockSpec(shape, lambda b, s: (0,) * len(shape), pipeline_mode=pl.Buffered(1))


def _conv_layer(h, g, w1, b1, wdw, bdw, lng, lnb, w2, b2, *, tile):
    bsz, seq, d = h.shape
    act = pl.BlockSpec((1, tile, d), lambda b, s: (b, s, 0))
    return pl.pallas_call(
        functools.partial(_conv_kernel, tile=tile, d=d),
        grid=(bsz, seq // tile),
        in_specs=[act, _const_spec((1, d)), _const_spec((d, 2 * d)), _const_spec((1, 2 * d)),
                  _const_spec((CONV_WIDTH, d)), _const_spec((1, d)), _const_spec((1, d)),
                  _const_spec((1, d)), _const_spec((d, d)), _const_spec((1, d))],
        out_specs=act,
        out_shape=jax.ShapeDtypeStruct(h.shape, h.dtype),
        scratch_shapes=[pltpu.VMEM((d // LANES, CONV_HALO + tile, LANES), _F32),
                        pltpu.VMEM((tile, d), _F32)],
        compiler_params=pltpu.CompilerParams(
            dimension_semantics=("arbitrary", "arbitrary"), vmem_limit_bytes=VMEM_LIMIT),
        name="conv_mixer",
    )(h, g, w1, b1, wdw, bdw, lng, lnb, w2, b2)


def _pool_kernel(h_ref, g_ref, w_ref, b_ref, sc_ref, o_ref, pbuf, *, tile, d):
    s = pl.program_id(1)
    gw = d // len(POOL_WINDOWS)
    h = h_ref[0]
    hn = _rmsnorm(h, g_ref[...])

    @pl.when(s == 0)
    def _():
        pbuf[0:POOL_HALO, :] = jnp.zeros((POOL_HALO, d), _F32)

    @pl.when(s > 0)
    def _():
        pbuf[0:POOL_HALO, :] = pbuf[tile:tile + POOL_HALO, :]

    pbuf[POOL_HALO:POOL_HALO + tile, :] = hn

    pos = s * tile + lax.broadcasted_iota(jnp.int32, (tile, 1), 0)
    for gi, win in enumerate(POOL_WINDOWS):
        cols = slice(gi * gw, (gi + 1) * gw)
        tot = hn[:, cols]
        for j in range(1, win):
            tot = tot + pbuf[POOL_HALO - j:POOL_HALO - j + tile, cols]
        cnt = jnp.minimum(pos + 1, win).astype(_F32)
        p = tot / cnt - hn[:, cols]
        y = _dot(p.astype(_BF16), w_ref[gi]) + b_ref[:, cols]
        o_ref[0, :, cols] = h[:, cols] + y * sc_ref[:, cols]


def _pool_layer(h, g, w, b, sc, *, tile):
    bsz, seq, d = h.shape
    n_grp, gw, _ = w.shape
    act = pl.BlockSpec((1, tile, d), lambda b, s: (b, s, 0))
    return pl.pallas_call(
        functools.partial(_pool_kernel, tile=tile, d=d),
        grid=(bsz, seq // tile),
        in_specs=[act, _const_spec((1, d)), _const_spec((n_grp, gw, gw)), _const_spec((1, d)),
                  _const_spec((1, d))],
        out_specs=act,
        out_shape=jax.ShapeDtypeStruct(h.shape, h.dtype),
        scratch_shapes=[pltpu.VMEM((POOL_HALO + tile, d), _F32)],
        compiler_params=pltpu.CompilerParams(
            dimension_semantics=("arbitrary", "arbitrary"), vmem_limit_bytes=VMEM_LIMIT),
        name="pool_mixer",
    )(h, g, w, b, sc)


def _mlp_kernel(h_ref, g_ref, w1_ref, w2_ref, gf_ref, o_ref, *, f_chunk, final_norm):
    h = h_ref[0]
    hb = _rmsnorm(h, g_ref[...]).astype(_BF16)
    acc = h
    for c in range(w1_ref.shape[1] // f_chunk):
        cols = slice(c * f_chunk, (c + 1) * f_chunk)
        u = jnp.maximum(_dot(hb, w1_ref[:, cols]), 0.0)
        acc = acc + _dot((u * u).astype(_BF16), w2_ref[cols, :])
    if final_norm:
        acc = _rmsnorm(acc, gf_ref[...])
    o_ref[0] = acc


def _mlp_layer(h, g, w1, w2, gf, *, tile, f_chunk, final_norm):
    bsz, seq, d = h.shape
    f = w1.shape[1]
    act = pl.BlockSpec((1, tile, d), lambda b, s: (b, s, 0))
    return pl.pallas_call(
        functools.partial(_mlp_kernel, f_chunk=f_chunk, final_norm=final_norm),
        grid=(bsz, seq // tile),
        in_specs=[act, _const_spec((1, d)), _const_spec((d, f)), _const_spec((f, d)),
                  _const_spec((1, d))],
        out_specs=act,
        out_shape=jax.ShapeDtypeStruct(h.shape, h.dtype),
        compiler_params=pltpu.CompilerParams(
            dimension_semantics=("arbitrary", "arbitrary"), vmem_limit_bytes=VMEM_LIMIT),
        name="sqrelu_mlp",
    )(h, g, w1, w2, gf)


def kernel(x, norm_mix, norm_mlp, conv_w1, conv_b1, conv_dw, conv_dwb, conv_ln_g, conv_ln_b,
           conv_w2, conv_b2, pool_w, pool_b, pool_scale, mlp_w1, mlp_w2, final_norm):
    depth, d = norm_mix.shape
    row = lambda v: v.reshape(1, -1)
    h = x
    ic = ip = 0
    for i in range(depth):
        if i % 2 == 0:
            h = _conv_layer(h, row(norm_mix[i]), conv_w1[ic].astype(_BF16), row(conv_b1[ic]),
                            conv_dw[ic], row(conv_dwb[ic]), row(conv_ln_g[ic]),
                            row(conv_ln_b[ic]), conv_w2[ic].astype(_BF16), row(conv_b2[ic]),
                            tile=512)
            ic += 1
        else:
            h = _pool_layer(h, row(norm_mix[i]), pool_w[ip].astype(_BF16),
                            row(pool_b[ip].reshape(-1)), row(pool_scale[ip]), tile=512)
            ip += 1
        h = _mlp_layer(h, row(norm_mlp[i]), mlp_w1[i].astype(_BF16), mlp_w2[i].astype(_BF16),
                       row(final_norm), tile=1024, f_chunk=1024, final_norm=(i == depth - 1))
    return h
```

```python
import functools

import jax
import jax.numpy as jnp
from jax import lax
from jax.experimental import pallas as pl
from jax.experimental.pallas import tpu as pltpu

EPS = 1e-6
CONV_WIDTH = 31
POOL_WINDOWS = (2, 4, 8, 16)

LANES = 128
CONV_HALO = 32
POOL_HALO = 16
CONV_ROWS = 32
MLP_F_CHUNK = 1024
TILE = 512
VMEM_LIMIT = 60 * 1024 * 1024

_F32 = jnp.float32
_BF16 = jnp.bfloat16


def _rmsnorm(x, g):
    ms = jnp.mean(x * x, axis=-1, keepdims=True)
    return x * lax.rsqrt(ms + EPS) * g


def _layernorm(x, g, b):
    mu = jnp.mean(x, axis=-1, keepdims=True)
    xc = x - mu
    var = jnp.mean(xc * xc, axis=-1, keepdims=True)
    return xc * lax.rsqrt(var + EPS) * g + b


def _dot(a, b):
    return jnp.dot(a, b, preferred_element_type=_F32)


def _interleave(*gens):
    gens = list(gens)
    while gens:
        for g in list(gens):
            try:
                next(g)
            except StopIteration:
                gens.remove(g)


def _mlp_steps(mid, g_ref, w1_ref, w2_ref, gf_ref, o_ref, final_norm):
    h = mid[...]
    hb = _rmsnorm(h, g_ref[...]).astype(_BF16)
    acc = h
    for c in range(w1_ref.shape[1] // MLP_F_CHUNK):
        cols = slice(c * MLP_F_CHUNK, (c + 1) * MLP_F_CHUNK)
        u = jnp.maximum(_dot(hb, w1_ref[:, cols]), 0.0)
        acc = acc + _dot((u * u).astype(_BF16), w2_ref[cols, :])
        yield
    if final_norm:
        acc = _rmsnorm(acc, gf_ref[...])
    o_ref[0] = acc


def _conv_layer_kernel(x_ref, g_ref, w1_ref, b1_ref, wdw_ref, bdw_ref, lng_ref, lnb_ref,
                       w2_ref, b2_ref, gm_ref, m1_ref, m2_ref, gf_ref, o_ref,
                       mid, ubuf, cbuf, hb, *, tile, d, seq_tiles, final_norm):
    i = pl.program_id(0)
    n_slab = d // LANES

    @pl.when(i == 0)
    def _():
        mid[...] = jnp.zeros(mid.shape, _F32)
        ubuf[:, tile:tile + CONV_HALO, :] = jnp.zeros((n_slab, CONV_HALO, LANES), _F32)

    hn = _rmsnorm(x_ref[0], g_ref[...])
    u = _dot(hn.astype(_BF16), w1_ref[...]) + b1_ref[...]
    glu = u[:, :d] * jax.nn.sigmoid(u[:, d:])
    seq_start = (i % seq_tiles) == 0
    tail = ubuf[:, tile:tile + CONV_HALO, :]
    ubuf[:, 0:CONV_HALO, :] = jnp.where(seq_start, 0.0, tail)
    for c in range(n_slab):
        ubuf[c, CONV_HALO:CONV_HALO + tile, :] = glu[:, c * LANES:(c + 1) * LANES]
    hb[...] = _rmsnorm(mid[...], gm_ref[...]).astype(_BF16)

    base = CONV_HALO - (CONV_WIDTH - 1)

    def conv_rows(r0):
        for c in range(n_slab):
            lanes = slice(c * LANES, (c + 1) * LANES)
            acc = jnp.broadcast_to(bdw_ref[:, lanes], (CONV_ROWS, LANES))
            for k in range(CONV_WIDTH):
                lo = r0 + base + k
                acc = acc + wdw_ref[k:k + 1, lanes] * ubuf[c, lo:lo + CONV_ROWS, :]
            cbuf[r0:r0 + CONV_ROWS, lanes] = acc

    n_phase = m1_ref.shape[1] // MLP_F_CHUNK
    rows_per_phase = tile // n_phase
    for p in range(n_phase):
        @pl.when(i >= 0)
        def _(p=p):
            cols = slice(p * MLP_F_CHUNK, (p + 1) * MLP_F_CHUNK)
            a = jnp.maximum(_dot(hb[...], m1_ref[:, cols]), 0.0)
            y = _dot((a * a).astype(_BF16), m2_ref[cols, :])
            o_ref[0] = (mid[...] if p == 0 else o_ref[0]) + y
            for r0 in range(p * rows_per_phase, (p + 1) * rows_per_phase, CONV_ROWS):
                conv_rows(r0)

    v = _layernorm(cbuf[...], lng_ref[...], lnb_ref[...])
    v = v * jax.nn.sigmoid(v)
    mid[...] = x_ref[0] + _dot(v.astype(_BF16), w2_ref[...]) + b2_ref[...]
    if final_norm:
        o_ref[0] = _rmsnorm(o_ref[0], gf_ref[...])


def _pool_layer_kernel(x_ref, g_ref, w_ref, b_ref, sc_ref, gm_ref, m1_ref, m2_ref, gf_ref,
                       o_ref, mid, pbuf, *, tile, d, seq_tiles, final_norm):
    i = pl.program_id(0)
    n_slab = d // LANES
    gw = d // len(POOL_WINDOWS)

    @pl.when(i == 0)
    def _():
        mid[...] = jnp.zeros(mid.shape, _F32)
        pbuf[:, tile:tile + POOL_HALO, :] = jnp.zeros((n_slab, POOL_HALO, LANES), _F32)

    x = x_ref[0]
    hn = _rmsnorm(x, g_ref[...])
    seq_start = (i % seq_tiles) == 0
    tail = pbuf[:, tile:tile + POOL_HALO, :]
    pbuf[:, 0:POOL_HALO, :] = jnp.where(seq_start, 0.0, tail)
    for c in range(n_slab):
        pbuf[c, POOL_HALO:POOL_HALO + tile, :] = hn[:, c * LANES:(c + 1) * LANES]

    pos = (i % seq_tiles) * tile + lax.broadcasted_iota(jnp.int32, (tile, 1), 0)

    def pool_steps():
        for gi, win in enumerate(POOL_WINDOWS):
            cols = slice(gi * gw, (gi + 1) * gw)
            inv_cnt = 1.0 / jnp.minimum(pos + 1, win).astype(_F32)
            parts = []
            for c in range(gi * gw // LANES, (gi + 1) * gw // LANES):
                tot = pbuf[c, POOL_HALO:POOL_HALO + tile, :]
                for j in range(1, win):
                    tot = tot + pbuf[c, POOL_HALO - j:POOL_HALO - j + tile, :]
                parts.append(tot)
            p = jnp.concatenate(parts, axis=-1) * inv_cnt - hn[:, cols]
            y = _dot(p.astype(_BF16), w_ref[gi]) + b_ref[:, cols]
            mid_next[gi] = x[:, cols] + y * sc_ref[:, cols]
            yield

    mid_next = [None] * len(POOL_WINDOWS)
    _interleave(_mlp_steps(mid, gm_ref, m1_ref, m2_ref, gf_ref, o_ref, final_norm),
                pool_steps())
    for gi in range(len(POOL_WINDOWS)):
        mid[:, gi * gw:(gi + 1) * gw] = mid_next[gi]


def _const_spec(shape):
    return pl.BlockSpec(shape, lambda i: (0,) * len(shape), pipeline_mode=pl.Buffered(1))


def _layer_call(body, x, params, scratch, *, tile, final_norm, name):
    bsz, seq, d = x.shape
    seq_tiles = seq // tile
    n_tiles = bsz * seq_tiles

    def in_map(i):
        j = jnp.minimum(i, n_tiles - 1)
        return (j // seq_tiles, j % seq_tiles, 0)

    def out_map(i):
        j = jnp.maximum(i - 1, 0)
        return (j // seq_tiles, j % seq_tiles, 0)

    return pl.pallas_call(
        functools.partial(body, tile=tile, d=d, seq_tiles=seq_tiles, final_norm=final_norm),
        grid=(n_tiles + 1,),
        in_specs=[pl.BlockSpec((1, tile, d), in_map)] + [_const_spec(p.shape) for p in params],
        out_specs=pl.BlockSpec((1, tile, d), out_map),
        out_shape=jax.ShapeDtypeStruct(x.shape, x.dtype),
        scratch_shapes=[pltpu.VMEM((tile, d), _F32)] + scratch,
        compiler_params=pltpu.CompilerParams(
            dimension_semantics=("arbitrary",), vmem_limit_bytes=VMEM_LIMIT),
        name=name,
    )(x, *params)


def kernel(x, norm_mix, norm_mlp, conv_w1, conv_b1, conv_dw, conv_dwb, conv_ln_g, conv_ln_b,
           conv_w2, conv_b2, pool_w, pool_b, pool_scale, mlp_w1, mlp_w2, final_norm):
    depth, d = norm_mix.shape
    tile = TILE
    row = lambda v: v.reshape(1, -1)
    h = x
    ic = ip = 0
    for i in range(depth):
        mlp = [row(norm_mlp[i]), mlp_w1[i].astype(_BF16), mlp_w2[i].astype(_BF16),
               row(final_norm)]
        last = i == depth - 1
        if i % 2 == 0:
            params = [row(norm_mix[i]), conv_w1[ic].astype(_BF16), row(conv_b1[ic]), conv_dw[ic],
                      row(conv_dwb[ic]), row(conv_ln_g[ic]), row(conv_ln_b[ic]),
                      conv_w2[ic].astype(_BF16), row(conv_b2[ic])] + mlp
            scratch = [pltpu.VMEM((d // LANES, CONV_HALO + tile, LANES), _F32),
                       pltpu.VMEM((tile, d), _F32), pltpu.VMEM((tile, d), _BF16)]
            h = _layer_call(_conv_layer_kernel, h, params, scratch, tile=tile,
                            final_norm=last, name="conv_layer")
            ic += 1
        else:
            params = [row(norm_mix[i]), pool_w[ip].astype(_BF16), row(pool_b[ip].reshape(-1)),
                      row(pool_scale[ip])] + mlp
            scratch = [pltpu.VMEM((d // LANES, POOL_HALO + tile, LANES), _F32)]
            h = _layer_call(_pool_layer_kernel, h, params, scratch, tile=tile,
                            final_norm=last, name="pool_layer")
            ip += 1
    return h
```

```python
import functools

import jax
import jax.numpy as jnp
from jax import lax
from jax.experimental import pallas as pl
from jax.experimental.pallas import tpu as pltpu

EPS = 1e-6
CONV_WIDTH = 31
POOL_WINDOWS = (2, 4, 8, 16)

LANES = 128
CONV_HALO = 32
POOL_HALO = 16
CONV_ROWS = 64
MLP_F_CHUNK = 1024
TILE = 512
STAGE_ROWS = 512
STAGE_COLS = 1024
VMEM_LIMIT = 60 * 1024 * 1024

_F32 = jnp.float32
_BF16 = jnp.bfloat16


def _rmsnorm(x, g):
    ms = jnp.mean(x * x, axis=-1, keepdims=True)
    return x * lax.rsqrt(ms + EPS) * g


def _layernorm(x, g, b):
    mu = jnp.mean(x, axis=-1, keepdims=True)
    xc = x - mu
    var = jnp.mean(xc * xc, axis=-1, keepdims=True)
    return xc * lax.rsqrt(var + EPS) * g + b


def _dot(a, b):
    return jnp.dot(a, b, preferred_element_type=_F32)


def _interleave(*gens):
    gens = list(gens)
    while gens:
        for g in list(gens):
            try:
                next(g)
            except StopIteration:
                gens.remove(g)


def _weight_chunks(src, dst, n_rows, n_cols):
    out = []
    for c in range(n_cols // STAGE_COLS):
        for r in range(0, n_rows, STAGE_ROWS):
            s = src.at[pl.ds(r, STAGE_ROWS), pl.ds(c * STAGE_COLS, STAGE_COLS)]
            if n_cols == STAGE_COLS:
                d = dst.at[pl.ds(r, STAGE_ROWS), :]
            else:
                d = dst.at[c, pl.ds(r, STAGE_ROWS), :]
            out.append((s, d))
    return out


def _load_weights(chunks, stage, sem):
    def copy(k):
        return pltpu.make_async_copy(chunks[k][0], stage.at[k % 2], sem.at[k % 2])

    copy(0).start()
    for k in range(len(chunks)):
        if k + 1 < len(chunks):
            copy(k + 1).start()
        copy(k).wait()
        chunks[k][1][...] = stage[k % 2].astype(_BF16)


def _mlp_steps(mid, g_ref, m1s, m2s, gf_ref, o_ref, final_norm):
    h = mid[...]
    hb = _rmsnorm(h, g_ref[...]).astype(_BF16)
    acc = h
    for c in range(m1s.shape[0]):
        u = jnp.maximum(_dot(hb, m1s[c]), 0.0)
        acc = acc + _dot((u * u).astype(_BF16), m2s[c * MLP_F_CHUNK:(c + 1) * MLP_F_CHUNK, :])
        yield
    if final_norm:
        acc = _rmsnorm(acc, gf_ref[...])
    o_ref[0] = acc


def _conv_layer_kernel(x_ref, g_ref, b1_ref, wdw_ref, bdw_ref, lng_ref, lnb_ref, b2_ref,
                       gm_ref, gf_ref, w1_hbm, w2_hbm, m1_hbm, m2_hbm, o_ref,
                       mid, ubuf, cbuf, w1s, w2s, m1s, m2s, stage, sem,
                       *, tile, d, seq_tiles, n_tiles, final_norm, mixer_idx, mlp_idx):
    i = pl.program_id(0)
    n_slab = d // LANES
    f = m2s.shape[0]

    @pl.when(i == 0)
    def _():
        ubuf[:, tile:tile + CONV_HALO, :] = jnp.zeros((n_slab, CONV_HALO, LANES), _F32)
        _load_weights(_weight_chunks(w1_hbm.at[mixer_idx], w1s, d, 2 * d)
                      + _weight_chunks(w2_hbm.at[mixer_idx], w2s, d, d)
                      + _weight_chunks(m1_hbm.at[mlp_idx], m1s, d, f)
                      + _weight_chunks(m2_hbm.at[mlp_idx], m2s, f, d), stage, sem)

    @pl.when(i > 0)
    def _():
        for _ in _mlp_steps(mid, gm_ref, m1s, m2s, gf_ref, o_ref, final_norm):
            pass

    @pl.when(i < n_tiles)
    def _():
        x = x_ref[0]
        hn = _rmsnorm(x, g_ref[...]).astype(_BF16)
        glu = (_dot(hn, w1s[0]) + b1_ref[:, :d]) * jax.nn.sigmoid(_dot(hn, w1s[1]) + b1_ref[:, d:])
        seq_start = (i % seq_tiles) == 0
        tail = ubuf[:, tile:tile + CONV_HALO, :]
        ubuf[:, 0:CONV_HALO, :] = jnp.where(seq_start, 0.0, tail)
        for c in range(n_slab):
            ubuf[c, CONV_HALO:CONV_HALO + tile, :] = glu[:, c * LANES:(c + 1) * LANES]

        base = CONV_HALO - (CONV_WIDTH - 1)

        def conv_rows(j, carry):
            r0 = pl.multiple_of(j * CONV_ROWS, CONV_ROWS)
            for c in range(n_slab):
                lanes = slice(c * LANES, (c + 1) * LANES)
                acc = jnp.broadcast_to(bdw_ref[:, lanes], (CONV_ROWS, LANES))
                for k in range(CONV_WIDTH):
                    acc = acc + wdw_ref[k:k + 1, lanes] * ubuf[c, pl.ds(r0 + (base + k), CONV_ROWS), :]
                cbuf[pl.ds(r0, CONV_ROWS), lanes] = acc
            return carry

        lax.fori_loop(0, tile // CONV_ROWS, conv_rows, 0)

        v = _layernorm(cbuf[...], lng_ref[...], lnb_ref[...])
        v = v * jax.nn.sigmoid(v)
        mid[...] = x_ref[0] + _dot(v.astype(_BF16), w2s[...]) + b2_ref[...]


def _pool_layer_kernel(x_ref, g_ref, w_ref, b_ref, sc_ref, gm_ref, gf_ref, m1_hbm, m2_hbm,
                       o_ref, mid, pbuf, m1s, m2s, stage, sem,
                       *, tile, d, seq_tiles, n_tiles, final_norm, mixer_idx, mlp_idx):
    del n_tiles, mixer_idx
    i = pl.program_id(0)
    n_slab = d // LANES
    gw = d // len(POOL_WINDOWS)
    f = m2s.shape[0]

    @pl.when(i == 0)
    def _():
        mid[...] = jnp.zeros(mid.shape, _F32)
        pbuf[:, tile:tile + POOL_HALO, :] = jnp.zeros((n_slab, POOL_HALO, LANES), _F32)
        _load_weights(_weight_chunks(m1_hbm.at[mlp_idx], m1s, d, f)
                      + _weight_chunks(m2_hbm.at[mlp_idx], m2s, f, d), stage, sem)

    x = x_ref[0]
    hn = _rmsnorm(x, g_ref[...])
    seq_start = (i % seq_tiles) == 0
    tail = pbuf[:, tile:tile + POOL_HALO, :]
    pbuf[:, 0:POOL_HALO, :] = jnp.where(seq_start, 0.0, tail)
    for c in range(n_slab):
        pbuf[c, POOL_HALO:POOL_HALO + tile, :] = hn[:, c * LANES:(c + 1) * LANES]

    pos = (i % seq_tiles) * tile + lax.broadcasted_iota(jnp.int32, (tile, 1), 0)
    mid_next = [None] * len(POOL_WINDOWS)

    def pool_steps():
        for gi, win in enumerate(POOL_WINDOWS):
            cols = slice(gi * gw, (gi + 1) * gw)
            inv_cnt = 1.0 / jnp.minimum(pos + 1, win).astype(_F32)
            parts = []
            for c in range(gi * gw // LANES, (gi + 1) * gw // LANES):
                tot = pbuf[c, POOL_HALO:POOL_HALO + tile, :]
                for j in range(1, win):
                    tot = tot + pbuf[c, POOL_HALO - j:POOL_HALO - j + tile, :]
                parts.append(tot)
            p = jnp.concatenate(parts, axis=-1) * inv_cnt - hn[:, cols]
            y = _dot(p.astype(_BF16), w_ref[gi]) + b_ref[:, cols]
            mid_next[gi] = x[:, cols] + y * sc_ref[:, cols]
            yield

    _interleave(_mlp_steps(mid, gm_ref, m1s, m2s, gf_ref, o_ref, final_norm), pool_steps())
    for gi in range(len(POOL_WINDOWS)):
        mid[:, gi * gw:(gi + 1) * gw] = mid_next[gi]


def _const_spec(shape):
    return pl.BlockSpec(shape, lambda i: (0,) * len(shape), pipeline_mode=pl.Buffered(1))


def _layer_call(body, x, params, hbm_weights, scratch, *, tile, final_norm, mixer_idx, mlp_idx,
                name):
    bsz, seq, d = x.shape
    seq_tiles = seq // tile
    n_tiles = bsz * seq_tiles

    def in_map(i):
        j = jnp.minimum(i, n_tiles - 1)
        return (j // seq_tiles, j % seq_tiles, 0)

    def out_map(i):
        j = jnp.maximum(i - 1, 0)
        return (j // seq_tiles, j % seq_tiles, 0)

    return pl.pallas_call(
        functools.partial(body, tile=tile, d=d, seq_tiles=seq_tiles, n_tiles=n_tiles,
                          final_norm=final_norm, mixer_idx=mixer_idx, mlp_idx=mlp_idx),
        grid=(n_tiles + 1,),
        in_specs=([pl.BlockSpec((1, tile, d), in_map)] + [_const_spec(p.shape) for p in params]
                  + [pl.BlockSpec(memory_space=pl.ANY)] * len(hbm_weights)),
        out_specs=pl.BlockSpec((1, tile, d), out_map),
        out_shape=jax.ShapeDtypeStruct(x.shape, x.dtype),
        scratch_shapes=[pltpu.VMEM((tile, d), _F32)] + scratch + [
            pltpu.VMEM((2, STAGE_ROWS, STAGE_COLS), _F32), pltpu.SemaphoreType.DMA((2,))],
        compiler_params=pltpu.CompilerParams(
            dimension_semantics=("arbitrary",), vmem_limit_bytes=VMEM_LIMIT),
        name=name,
    )(x, *params, *hbm_weights)


def kernel(x, norm_mix, norm_mlp, conv_w1, conv_b1, conv_dw, conv_dwb, conv_ln_g, conv_ln_b,
           conv_w2, conv_b2, pool_w, pool_b, pool_scale, mlp_w1, mlp_w2, final_norm):
    depth, d = norm_mix.shape
    f = mlp_w1.shape[-1]
    tile = TILE
    row = lambda v: v.reshape(1, -1)
    mlp_scratch = [pltpu.VMEM((f // MLP_F_CHUNK, d, MLP_F_CHUNK), _BF16), pltpu.VMEM((f, d), _BF16)]
    h = x
    ic = ip = 0
    for i in range(depth):
        last = i == depth - 1
        if i % 2 == 0:
            params = [row(norm_mix[i]), row(conv_b1[ic]), conv_dw[ic], row(conv_dwb[ic]),
                      row(conv_ln_g[ic]), row(conv_ln_b[ic]), row(conv_b2[ic]),
                      row(norm_mlp[i]), row(final_norm)]
            scratch = [pltpu.VMEM((d // LANES, CONV_HALO + tile, LANES), _F32),
                       pltpu.VMEM((tile, d), _F32),
                       pltpu.VMEM((2, d, d), _BF16), pltpu.VMEM((d, d), _BF16)] + mlp_scratch
            h = _layer_call(_conv_layer_kernel, h, params,
                            [conv_w1, conv_w2, mlp_w1, mlp_w2], scratch, tile=tile,
                            final_norm=last, mixer_idx=ic, mlp_idx=i, name="conv_layer")
            ic += 1
        else:
            params = [row(norm_mix[i]), pool_w[ip].astype(_BF16), row(pool_b[ip].reshape(-1)),
                      row(pool_scale[ip]), row(norm_mlp[i]), row(final_norm)]
            scratch = [pltpu.VMEM((d // LANES, POOL_HALO + tile, LANES), _F32)] + mlp_scratch
            h = _layer_call(_pool_layer_kernel, h, params, [mlp_w1, mlp_w2], scratch, tile=tile,
                            final_norm=last, mixer_idx=ip, mlp_idx=i, name="pool_layer")
            ip += 1
    return h
```

```python
import functools

import jax
import jax.numpy as jnp
from jax import lax
from jax.experimental import pallas as pl
from jax.experimental.pallas import tpu as pltpu

EPS = 1e-6
CONV_WIDTH = 31
POOL_WINDOWS = (2, 4, 8, 16)

LANES = 128
CONV_HALO = 32
POOL_HALO = 16
CONV_ROWS = 64
MLP_F_CHUNK = 1024
TILE = 512
STAGE_ROWS = 512
STAGE_COLS = 1024
VMEM_LIMIT = 60 * 1024 * 1024

_F32 = jnp.float32
_BF16 = jnp.bfloat16


def _rmsnorm(x, g):
    ms = jnp.mean(x * x, axis=-1, keepdims=True)
    return x * lax.rsqrt(ms + EPS) * g


def _layernorm(x, g, b):
    mu = jnp.mean(x, axis=-1, keepdims=True)
    xc = x - mu
    var = jnp.mean(xc * xc, axis=-1, keepdims=True)
    return xc * lax.rsqrt(var + EPS) * g + b


def _dot(a, b):
    return jnp.dot(a, b, preferred_element_type=_F32)


def _interleave(*gens):
    gens = list(gens)
    while gens:
        for g in list(gens):
            try:
                next(g)
            except StopIteration:
                gens.remove(g)


_ALL = slice(None)


def _rows(r):
    return pl.ds(r, STAGE_ROWS)


def _weight_chunks(src, n_rows, n_cols, dst_fn):
    return [(src.at[_rows(r), pl.ds(c, STAGE_COLS)], dst_fn(r, c))
            for c in range(0, n_cols, STAGE_COLS) for r in range(0, n_rows, STAGE_ROWS)]


def _load_weights(chunks, stage, sem):
    def copy(k):
        return pltpu.make_async_copy(chunks[k][0], stage.at[k % 2], sem.at[k % 2])

    copy(0).start()
    for k in range(len(chunks)):
        if k + 1 < len(chunks):
            copy(k + 1).start()
        copy(k).wait()
        for rows, cols, dst in chunks[k][1]:
            dst[...] = stage[k % 2, rows, cols].astype(_BF16)


def _mlp_steps(mid, g_ref, m1s, m2s, gf_ref, o_ref, final_norm):
    h = mid[...]
    hb = _rmsnorm(h, g_ref[...]).astype(_BF16)
    acc = h
    for c in range(m1s.shape[0]):
        u = jnp.maximum(_dot(hb, m1s[c]), 0.0)
        acc = acc + _dot((u * u).astype(_BF16), m2s[c * MLP_F_CHUNK:(c + 1) * MLP_F_CHUNK, :])
        yield
    if final_norm:
        acc = _rmsnorm(acc, gf_ref[...])
    o_ref[0] = acc


def _conv_layer_kernel(x_ref, g_ref, b1_ref, wdw_ref, bdw_ref, lng_ref, lnb_ref, b2_ref,
                       gm_ref, gf_ref, w1_hbm, w2_hbm, m1_hbm, m2_hbm, o_ref,
                       mid, ubuf, cbuf, hb, w1s, w2s, m1s, m2s, stage, sem,
                       *, tile, d, seq_tiles, n_tiles, final_norm, mixer_idx, mlp_idx):
    i = pl.program_id(0)
    n_slab = d // LANES
    n_j, _, fw = m1s.shape
    f = n_j * fw
    base = CONV_HALO - (CONV_WIDTH - 1)

    @pl.when(i == 0)
    def _():
        ubuf[:, tile:tile + CONV_HALO, :] = jnp.zeros((n_slab, CONV_HALO, LANES), _F32)

        def m1_dst(r, c):
            return [(_ALL, slice(s, s + fw), m1s.at[(c + s) // fw, _rows(r), :])
                    for s in range(0, STAGE_COLS, fw)]

        def m2_dst(r, c):
            return [(slice(t, t + fw), _ALL, m2s.at[(r + t) // fw])
                    for t in range(0, STAGE_ROWS, fw)]

        _load_weights(
            _weight_chunks(w1_hbm.at[mixer_idx], d, 2 * d,
                           lambda r, c: [(_ALL, _ALL, w1s.at[c // STAGE_COLS, _rows(r), :])])
            + _weight_chunks(w2_hbm.at[mixer_idx], d, d,
                             lambda r, c: [(_ALL, _ALL, w2s.at[_rows(r), :])])
            + _weight_chunks(m1_hbm.at[mlp_idx], d, f, m1_dst)
            + _weight_chunks(m2_hbm.at[mlp_idx], f, d, m2_dst), stage, sem)

    @pl.when(i > 0)
    def _():
        h = mid[...]
        hb[...] = _rmsnorm(h, gm_ref[...]).astype(_BF16)
        o_ref[0] = h

    @pl.when(i < n_tiles)
    def _():
        x = x_ref[0]
        hn = _rmsnorm(x, g_ref[...]).astype(_BF16)
        glu = (_dot(hn, w1s[0]) + b1_ref[:, :d]) * jax.nn.sigmoid(_dot(hn, w1s[1]) + b1_ref[:, d:])
        seq_start = (i % seq_tiles) == 0
        tail = ubuf[:, tile:tile + CONV_HALO, :]
        ubuf[:, 0:CONV_HALO, :] = jnp.where(seq_start, 0.0, tail)
        for c in range(n_slab):
            ubuf[c, CONV_HALO:CONV_HALO + tile, :] = glu[:, c * LANES:(c + 1) * LANES]

    def conv_rows(j):
        r0 = pl.multiple_of(j * CONV_ROWS, CONV_ROWS)
        for c in range(n_slab):
            lanes = slice(c * LANES, (c + 1) * LANES)
            acc = jnp.broadcast_to(bdw_ref[:, lanes], (CONV_ROWS, LANES))
            for k in range(CONV_WIDTH):
                acc = acc + wdw_ref[k:k + 1, lanes] * ubuf[c, pl.ds(r0 + (base + k), CONV_ROWS), :]
            cbuf[pl.ds(r0, CONV_ROWS), lanes] = acc

    def fused(j, carry):
        a = jnp.maximum(_dot(hb[...], m1s[j]), 0.0)
        o_ref[0] += _dot((a * a).astype(_BF16), m2s[j])
        conv_rows(j)
        return carry

    def conv_only(j, carry):
        conv_rows(j)
        return carry

    @pl.when(i == 0)
    def _():
        lax.fori_loop(0, n_j, conv_only, 0)

    @pl.when(i > 0)
    def _():
        lax.fori_loop(0, n_j, fused, 0)

    @pl.when(i < n_tiles)
    def _():
        v = _layernorm(cbuf[...], lng_ref[...], lnb_ref[...])
        v = v * jax.nn.sigmoid(v)
        mid[...] = x_ref[0] + _dot(v.astype(_BF16), w2s[...]) + b2_ref[...]

    if final_norm:
        @pl.when(i > 0)
        def _():
            o_ref[0] = _rmsnorm(o_ref[0], gf_ref[...])


def _pool_layer_kernel(x_ref, g_ref, w_ref, b_ref, sc_ref, gm_ref, gf_ref, m1_hbm, m2_hbm,
                       o_ref, mid, pbuf, m1s, m2s, stage, sem,
                       *, tile, d, seq_tiles, n_tiles, final_norm, mixer_idx, mlp_idx):
    del n_tiles, mixer_idx
    i = pl.program_id(0)
    n_slab = d // LANES
    gw = d // len(POOL_WINDOWS)
    f = m2s.shape[0]

    @pl.when(i == 0)
    def _():
        mid[...] = jnp.zeros(mid.shape, _F32)
        pbuf[:, tile:tile + POOL_HALO, :] = jnp.zeros((n_slab, POOL_HALO, LANES), _F32)
        _load_weights(
            _weight_chunks(m1_hbm.at[mlp_idx], d, f,
                           lambda r, c: [(_ALL, _ALL, m1s.at[c // STAGE_COLS, _rows(r), :])])
            + _weight_chunks(m2_hbm.at[mlp_idx], f, d,
                             lambda r, c: [(_ALL, _ALL, m2s.at[_rows(r), :])]), stage, sem)

    x = x_ref[0]
    hn = _rmsnorm(x, g_ref[...])
    seq_start = (i % seq_tiles) == 0
    tail = pbuf[:, tile:tile + POOL_HALO, :]
    pbuf[:, 0:POOL_HALO, :] = jnp.where(seq_start, 0.0, tail)
    for c in range(n_slab):
        pbuf[c, POOL_HALO:POOL_HALO + tile, :] = hn[:, c * LANES:(c + 1) * LANES]

    pos = (i % seq_tiles) * tile + lax.broadcasted_iota(jnp.int32, (tile, 1), 0)
    mid_next = [None] * len(POOL_WINDOWS)

    def pool_steps():
        for gi, win in enumerate(POOL_WINDOWS):
            cols = slice(gi * gw, (gi + 1) * gw)
            inv_cnt = 1.0 / jnp.minimum(pos + 1, win).astype(_F32)
            parts = []
            for c in range(gi * gw // LANES, (gi + 1) * gw // LANES):
                tot = pbuf[c, POOL_HALO:POOL_HALO + tile, :]
                for j in range(1, win):
                    tot = tot + pbuf[c, POOL_HALO - j:POOL_HALO - j + tile, :]
                parts.append(tot)
            p = jnp.concatenate(parts, axis=-1) * inv_cnt - hn[:, cols]
            y = _dot(p.astype(_BF16), w_ref[gi]) + b_ref[:, cols]
            mid_next[gi] = x[:, cols] + y * sc_ref[:, cols]
            yield

    _interleave(_mlp_steps(mid, gm_ref, m1s, m2s, gf_ref, o_ref, final_norm), pool_steps())
    for gi in range(len(POOL_WINDOWS)):
        mid[:, gi * gw:(gi + 1) * gw] = mid_next[gi]


def _const_spec(shape):
    return pl.BlockSpec(shape, lambda i: (0,) * len(shape), pipeline_mode=pl.Buffered(1))


def _layer_call(body, x, params, hbm_weights, scratch, *, tile, final_norm, mixer_idx, mlp_idx,
                name):
    bsz, seq, d = x.shape
    seq_tiles = seq // tile
    n_tiles = bsz * seq_tiles

    def in_map(i):
        j = jnp.minimum(i, n_tiles - 1)
        return (j // seq_tiles, j % seq_tiles, 0)

    def out_map(i):
        j = jnp.maximum(i - 1, 0)
        return (j // seq_tiles, j % seq_tiles, 0)

    return pl.pallas_call(
        functools.partial(body, tile=tile, d=d, seq_tiles=seq_tiles, n_tiles=n_tiles,
                          final_norm=final_norm, mixer_idx=mixer_idx, mlp_idx=mlp_idx),
        grid=(n_tiles + 1,),
        in_specs=([pl.BlockSpec((1, tile, d), in_map)] + [_const_spec(p.shape) for p in params]
                  + [pl.BlockSpec(memory_space=pl.ANY)] * len(hbm_weights)),
        out_specs=pl.BlockSpec((1, tile, d), out_map),
        out_shape=jax.ShapeDtypeStruct(x.shape, x.dtype),
        scratch_shapes=[pltpu.VMEM((tile, d), _F32)] + scratch + [
            pltpu.VMEM((2, STAGE_ROWS, STAGE_COLS), _F32), pltpu.SemaphoreType.DMA((2,))],
        compiler_params=pltpu.CompilerParams(
            dimension_semantics=("arbitrary",), vmem_limit_bytes=VMEM_LIMIT),
        name=name,
    )(x, *params, *hbm_weights)


def kernel(x, norm_mix, norm_mlp, conv_w1, conv_b1, conv_dw, conv_dwb, conv_ln_g, conv_ln_b,
           conv_w2, conv_b2, pool_w, pool_b, pool_scale, mlp_w1, mlp_w2, final_norm):
    depth, d = norm_mix.shape
    f = mlp_w1.shape[-1]
    tile = TILE
    row = lambda v: v.reshape(1, -1)
    mlp_scratch = [pltpu.VMEM((f // MLP_F_CHUNK, d, MLP_F_CHUNK), _BF16), pltpu.VMEM((f, d), _BF16)]
    n_trips = tile // CONV_ROWS
    h = x
    ic = ip = 0
    for i in range(depth):
        last = i == depth - 1
        if i % 2 == 0:
            params = [row(norm_mix[i]), row(conv_b1[ic]), conv_dw[ic], row(conv_dwb[ic]),
                      row(conv_ln_g[ic]), row(conv_ln_b[ic]), row(conv_b2[ic]),
                      row(norm_mlp[i]), row(final_norm)]
            scratch = [pltpu.VMEM((d // LANES, CONV_HALO + tile, LANES), _F32),
                       pltpu.VMEM((tile, d), _F32), pltpu.VMEM((tile, d), _BF16),
                       pltpu.VMEM((2, d, d), _BF16), pltpu.VMEM((d, d), _BF16),
                       pltpu.VMEM((n_trips, d, f // n_trips), _BF16),
                       pltpu.VMEM((n_trips, f // n_trips, d), _BF16)]
            h = _layer_call(_conv_layer_kernel, h, params,
                            [conv_w1, conv_w2, mlp_w1, mlp_w2], scratch, tile=tile,
                            final_norm=last, mixer_idx=ic, mlp_idx=i, name="conv_layer")
            ic += 1
        else:
            params = [row(norm_mix[i]), pool_w[ip].astype(_BF16), row(pool_b[ip].reshape(-1)),
                      row(pool_scale[ip]), row(norm_mlp[i]), row(final_norm)]
            scratch = [pltpu.VMEM((d // LANES, POOL_HALO + tile, LANES), _F32)] + mlp_scratch
            h = _layer_call(_pool_layer_kernel, h, params, [mlp_w1, mlp_w2], scratch, tile=tile,
                            final_norm=last, mixer_idx=ip, mlp_idx=i, name="pool_layer")
            ip += 1
    return h
```

```python
import functools

import jax
import jax.numpy as jnp
from jax import lax
from jax.experimental import pallas as pl
from jax.experimental.pallas import tpu as pltpu

EPS = 1e-6
CONV_WIDTH = 31
POOL_WINDOWS = (2, 4, 8, 16)

LANES = 128
CONV_HALO = 32
POOL_HALO = 16
CONV_ROWS = 64
MLP_F_CHUNK = 1024
TILE = 512
STAGE_ROWS = 512
STAGE_COLS = 1024
VMEM_LIMIT = 60 * 1024 * 1024

_F32 = jnp.float32
_BF16 = jnp.bfloat16


def _rmsnorm(x, g):
    ms = jnp.mean(x * x, axis=-1, keepdims=True)
    return x * lax.rsqrt(ms + EPS) * g


def _layernorm(x, g, b):
    mu = jnp.mean(x, axis=-1, keepdims=True)
    xc = x - mu
    var = jnp.mean(xc * xc, axis=-1, keepdims=True)
    return xc * lax.rsqrt(var + EPS) * g + b


def _dot(a, b):
    return jnp.dot(a, b, preferred_element_type=_F32)


def _row(ref, idx):
    return ref[idx:idx + 1, :]


def _interleave(*gens):
    gens = list(gens)
    while gens:
        for g in list(gens):
            try:
                next(g)
            except StopIteration:
                gens.remove(g)


def _weight_chunks(src, dst, n_rows, n_cols):
    out = []
    for c in range(n_cols // STAGE_COLS):
        for r in range(0, n_rows, STAGE_ROWS):
            s = src.at[pl.ds(r, STAGE_ROWS), pl.ds(c * STAGE_COLS, STAGE_COLS)]
            if n_cols == STAGE_COLS:
                d = dst.at[pl.ds(r, STAGE_ROWS), :]
            else:
                d = dst.at[c, pl.ds(r, STAGE_ROWS), :]
            out.append((s, d))
    return out


def _load_weights(chunks, stage, sem):
    def copy(k):
        return pltpu.make_async_copy(chunks[k][0], stage.at[k % 2], sem.at[k % 2])

    copy(0).start()
    for k in range(len(chunks)):
        if k + 1 < len(chunks):
            copy(k + 1).start()
        copy(k).wait()
        chunks[k][1][...] = stage[k % 2].astype(_BF16)


def _mlp_steps(mid, g, m1s, m2s, gf, o_ref, final_norm):
    h = mid[...]
    hb = _rmsnorm(h, g).astype(_BF16)
    acc = h
    for c in range(m1s.shape[0]):
        u = jnp.maximum(_dot(hb, m1s[c]), 0.0)
        acc = acc + _dot((u * u).astype(_BF16), m2s[c * MLP_F_CHUNK:(c + 1) * MLP_F_CHUNK, :])
        yield
    if final_norm:
        acc = _rmsnorm(acc, gf)
    o_ref[0] = acc


def _conv_layer_kernel(x_ref, g_ref, b1_ref, wdw_ref, bdw_ref, lng_ref, lnb_ref, b2_ref,
                       gm_ref, gf_ref, w1_hbm, w2_hbm, m1_hbm, m2_hbm, o_ref,
                       mid, ubuf, cbuf, w1s, w2s, m1s, m2s, stage, sem,
                       *, tile, d, seq_tiles, n_tiles, final_norm, mixer_idx, layer):
    i = pl.program_id(0)
    n_slab = d // LANES
    f = m2s.shape[0]

    @pl.when(i == 0)
    def _():
        ubuf[:, tile:tile + CONV_HALO, :] = jnp.zeros((n_slab, CONV_HALO, LANES), _F32)
        _load_weights(_weight_chunks(w1_hbm.at[mixer_idx], w1s, d, 2 * d)
                      + _weight_chunks(w2_hbm.at[mixer_idx], w2s, d, d)
                      + _weight_chunks(m1_hbm.at[layer], m1s, d, f)
                      + _weight_chunks(m2_hbm.at[layer], m2s, f, d), stage, sem)

    @pl.when(i > 0)
    def _():
        for _ in _mlp_steps(mid, _row(gm_ref, layer), m1s, m2s, gf_ref[...], o_ref, final_norm):
            pass

    @pl.when(i < n_tiles)
    def _():
        x = x_ref[0]
        hn = _rmsnorm(x, _row(g_ref, layer)).astype(_BF16)
        b1 = _row(b1_ref, mixer_idx)
        glu = (_dot(hn, w1s[0]) + b1[:, :d]) * jax.nn.sigmoid(_dot(hn, w1s[1]) + b1[:, d:])
        seq_start = (i % seq_tiles) == 0
        tail = ubuf[:, tile:tile + CONV_HALO, :]
        ubuf[:, 0:CONV_HALO, :] = jnp.where(seq_start, 0.0, tail)
        for c in range(n_slab):
            ubuf[c, CONV_HALO:CONV_HALO + tile, :] = glu[:, c * LANES:(c + 1) * LANES]

        base = CONV_HALO - (CONV_WIDTH - 1)

        def conv_rows(j, carry):
            r0 = pl.multiple_of(j * CONV_ROWS, CONV_ROWS)
            for c in range(n_slab):
                lanes = slice(c * LANES, (c + 1) * LANES)
                acc = jnp.broadcast_to(bdw_ref[mixer_idx:mixer_idx + 1, lanes], (CONV_ROWS, LANES))
                for k in range(CONV_WIDTH):
                    acc = acc + (wdw_ref[mixer_idx, k:k + 1, lanes]
                                 * ubuf[c, pl.ds(r0 + (base + k), CONV_ROWS), :])
                cbuf[pl.ds(r0, CONV_ROWS), lanes] = acc
            return carry

        lax.fori_loop(0, tile // CONV_ROWS, conv_rows, 0)

        v = _layernorm(cbuf[...], _row(lng_ref, mixer_idx), _row(lnb_ref, mixer_idx))
        v = v * jax.nn.sigmoid(v)
        mid[...] = x_ref[0] + _dot(v.astype(_BF16), w2s[...]) + _row(b2_ref, mixer_idx)


def _pool_layer_kernel(x_ref, g_ref, w_ref, b_ref, sc_ref, gm_ref, gf_ref, m1_hbm, m2_hbm,
                       o_ref, mid, pbuf, pws, m1s, m2s, stage, sem,
                       *, tile, d, seq_tiles, n_tiles, final_norm, mixer_idx, layer):
    del n_tiles
    i = pl.program_id(0)
    n_slab = d // LANES
    gw = d // len(POOL_WINDOWS)
    f = m2s.shape[0]

    @pl.when(i == 0)
    def _():
        pbuf[:, tile:tile + POOL_HALO, :] = jnp.zeros((n_slab, POOL_HALO, LANES), _F32)
        pws[...] = w_ref[mixer_idx].astype(_BF16)
        _load_weights(_weight_chunks(m1_hbm.at[layer], m1s, d, f)
                      + _weight_chunks(m2_hbm.at[layer], m2s, f, d), stage, sem)

    def step(with_mlp):
        x = x_ref[0]
        hn = _rmsnorm(x, _row(g_ref, layer))
        seq_start = (i % seq_tiles) == 0
        tail = pbuf[:, tile:tile + POOL_HALO, :]
        pbuf[:, 0:POOL_HALO, :] = jnp.where(seq_start, 0.0, tail)
        for c in range(n_slab):
            pbuf[c, POOL_HALO:POOL_HALO + tile, :] = hn[:, c * LANES:(c + 1) * LANES]

        pos = (i % seq_tiles) * tile + lax.broadcasted_iota(jnp.int32, (tile, 1), 0)
        mid_next = [None] * len(POOL_WINDOWS)
        bias = _row(b_ref, mixer_idx)
        scale = _row(sc_ref, mixer_idx)

        def pool_steps():
            for gi, win in enumerate(POOL_WINDOWS):
                cols = slice(gi * gw, (gi + 1) * gw)
                inv_cnt = 1.0 / jnp.minimum(pos + 1, win).astype(_F32)
                parts = []
                for c in range(gi * gw // LANES, (gi + 1) * gw // LANES):
                    tot = pbuf[c, POOL_HALO:POOL_HALO + tile, :]
                    for j in range(1, win):
                        tot = tot + pbuf[c, POOL_HALO - j:POOL_HALO - j + tile, :]
                    parts.append(tot)
                p = jnp.concatenate(parts, axis=-1) * inv_cnt - hn[:, cols]
                y = _dot(p.astype(_BF16), pws[gi]) + bias[:, cols]
                mid_next[gi] = x[:, cols] + y * scale[:, cols]
                yield

        gens = [pool_steps()]
        if with_mlp:
            gens.insert(0, _mlp_steps(mid, _row(gm_ref, layer), m1s, m2s, gf_ref[...], o_ref,
                                      final_norm))
        _interleave(*gens)
        for gi in range(len(POOL_WINDOWS)):
            mid[:, gi * gw:(gi + 1) * gw] = mid_next[gi]

    @pl.when(i == 0)
    def _():
        step(False)

    @pl.when(i > 0)
    def _():
        step(True)


def _const_spec(shape):
    return pl.BlockSpec(shape, lambda i: (0,) * len(shape), pipeline_mode=pl.Buffered(1))


def _layer_call(body, x, params, hbm_weights, scratch, *, tile, final_norm, mixer_idx, layer,
                name):
    bsz, seq, d = x.shape
    seq_tiles = seq // tile
    n_tiles = bsz * seq_tiles

    def in_map(i):
        j = jnp.minimum(i, n_tiles - 1)
        return (j // seq_tiles, j % seq_tiles, 0)

    def out_map(i):
        j = jnp.maximum(i - 1, 0)
        return (j // seq_tiles, j % seq_tiles, 0)

    return pl.pallas_call(
        functools.partial(body, tile=tile, d=d, seq_tiles=seq_tiles, n_tiles=n_tiles,
                          final_norm=final_norm, mixer_idx=mixer_idx, layer=layer),
        grid=(n_tiles + 1,),
        in_specs=([pl.BlockSpec((1, tile, d), in_map)] + [_const_spec(p.shape) for p in params]
                  + [pl.BlockSpec(memory_space=pl.ANY)] * len(hbm_weights)),
        out_specs=pl.BlockSpec((1, tile, d), out_map),
        out_shape=jax.ShapeDtypeStruct(x.shape, x.dtype),
        scratch_shapes=[pltpu.VMEM((tile, d), _F32)] + scratch + [
            pltpu.VMEM((2, STAGE_ROWS, STAGE_COLS), _F32), pltpu.SemaphoreType.DMA((2,))],
        compiler_params=pltpu.CompilerParams(
            dimension_semantics=("arbitrary",), vmem_limit_bytes=VMEM_LIMIT),
        name=name,
    )(x, *params, *hbm_weights)


def kernel(x, norm_mix, norm_mlp, conv_w1, conv_b1, conv_dw, conv_dwb, conv_ln_g, conv_ln_b,
           conv_w2, conv_b2, pool_w, pool_b, pool_scale, mlp_w1, mlp_w2, final_norm):
    depth, d = norm_mix.shape
    f = mlp_w1.shape[-1]
    tile = TILE
    gf = final_norm.reshape(1, d)
    pool_bias = pool_b.reshape(pool_b.shape[0], d)
    mlp_scratch = [pltpu.VMEM((f // MLP_F_CHUNK, d, MLP_F_CHUNK), _BF16), pltpu.VMEM((f, d), _BF16)]
    h = x
    ic = ip = 0
    for i in range(depth):
        last = i == depth - 1
        if i % 2 == 0:
            params = [norm_mix, conv_b1, conv_dw, conv_dwb, conv_ln_g, conv_ln_b, conv_b2,
                      norm_mlp, gf]
            scratch = [pltpu.VMEM((d // LANES, CONV_HALO + tile, LANES), _F32),
                       pltpu.VMEM((tile, d), _F32),
                       pltpu.VMEM((2, d, d), _BF16), pltpu.VMEM((d, d), _BF16)] + mlp_scratch
            h = _layer_call(_conv_layer_kernel, h, params,
                            [conv_w1, conv_w2, mlp_w1, mlp_w2], scratch, tile=tile,
                            final_norm=last, mixer_idx=ic, layer=i, name="conv_layer")
            ic += 1
        else:
            params = [norm_mix, pool_w, pool_bias, pool_scale, norm_mlp, gf]
            scratch = [pltpu.VMEM((d // LANES, POOL_HALO + tile, LANES), _F32),
                       pltpu.VMEM(pool_w.shape[1:], _BF16)] + mlp_scratch
            h = _layer_call(_pool_layer_kernel, h, params, [mlp_w1, mlp_w2], scratch, tile=tile,
                            final_norm=last, mixer_idx=ip, layer=i, name="pool_layer")
            ip += 1
    return h
```

```python
import functools

import jax
import jax.numpy as jnp
from jax import lax
from jax.experimental import pallas as pl
from jax.experimental.pallas import tpu as pltpu

EPS = 1e-6
CONV_WIDTH = 31
POOL_WINDOWS = (2, 4, 8, 16)

LANES = 128
CONV_HALO = 32
POOL_HALO = 16
CONV_ROWS = 64
MLP_F_CHUNK = 1024
TILE = 512
STAGE_ROWS = 512
STAGE_COLS = 1024
VMEM_LIMIT = 60 * 1024 * 1024

_F32 = jnp.float32
_BF16 = jnp.bfloat16


def _rmsnorm(x, g):
    ms = jnp.mean(x * x, axis=-1, keepdims=True)
    return x * lax.rsqrt(ms + EPS) * g


def _layernorm(x, g, b):
    mu = jnp.mean(x, axis=-1, keepdims=True)
    xc = x - mu
    var = jnp.mean(xc * xc, axis=-1, keepdims=True)
    return xc * lax.rsqrt(var + EPS) * g + b


def _dot(a, b):
    return jnp.dot(a, b, preferred_element_type=_F32)


def _row(ref, idx):
    return ref[idx:idx + 1, :]


def _interleave(*gens):
    gens = list(gens)
    while gens:
        for g in list(gens):
            try:
                next(g)
            except StopIteration:
                gens.remove(g)


def _weight_chunks(src, dst, n_rows, n_cols):
    out = []
    for c in range(n_cols // STAGE_COLS):
        for r in range(0, n_rows, STAGE_ROWS):
            s = src.at[pl.ds(r, STAGE_ROWS), pl.ds(c * STAGE_COLS, STAGE_COLS)]
            if n_cols == STAGE_COLS:
                d = dst.at[pl.ds(r, STAGE_ROWS), :]
            else:
                d = dst.at[c, pl.ds(r, STAGE_ROWS), :]
            out.append((s, d))
    return out


def _load_weights(chunks, stage, sem):
    def copy(k):
        return pltpu.make_async_copy(chunks[k][0], stage.at[k % 2], sem.at[k % 2])

    copy(0).start()
    for k in range(len(chunks)):
        if k + 1 < len(chunks):
            copy(k + 1).start()
        copy(k).wait()
        chunks[k][1][...] = stage[k % 2].astype(_BF16)


def _mlp_steps(mid, g, m1s, m2s, gf, o_ref, final_norm):
    h = mid[...]
    hb = _rmsnorm(h, g).astype(_BF16)
    acc = h
    for c in range(m1s.shape[0]):
        u = jnp.maximum(_dot(hb, m1s[c]), 0.0)
        acc = acc + _dot((u * u).astype(_BF16), m2s[c * MLP_F_CHUNK:(c + 1) * MLP_F_CHUNK, :])
        yield
    if final_norm:
        acc = _rmsnorm(acc, gf)
    o_ref[0] = acc


def _conv_layer_kernel(x_ref, g_ref, b1_ref, wdw_ref, bdw_ref, lng_ref, lnb_ref, b2_ref,
                       gm_ref, gf_ref, w1_hbm, w2_hbm, m1_hbm, m2_hbm, o_ref,
                       mid, ubuf, cbuf, w1s, w2s, m1s, m2s, stage, sem,
                       *, tile, d, seq_tiles, n_tiles, final_norm, mixer_idx, layer):
    i = pl.program_id(0)
    n_slab = d // LANES
    f = m2s.shape[0]

    @pl.when(i == 0)
    def _():
        ubuf[:, tile:tile + CONV_HALO, :] = jnp.zeros((n_slab, CONV_HALO, LANES), _F32)
        _load_weights(_weight_chunks(w1_hbm.at[mixer_idx], w1s, d, 2 * d)
                      + _weight_chunks(w2_hbm.at[mixer_idx], w2s, d, d)
                      + _weight_chunks(m1_hbm.at[layer], m1s, d, f)
                      + _weight_chunks(m2_hbm.at[layer], m2s, f, d), stage, sem)

    def mlp_block():
        for _ in _mlp_steps(mid, _row(gm_ref, layer), m1s, m2s, gf_ref[...], o_ref, final_norm):
            pass

    def mixer_part1():
        x = x_ref[0]
        hn = _rmsnorm(x, _row(g_ref, layer)).astype(_BF16)
        b1 = _row(b1_ref, mixer_idx)
        glu = (_dot(hn, w1s[0]) + b1[:, :d]) * jax.nn.sigmoid(_dot(hn, w1s[1]) + b1[:, d:])
        seq_start = (i % seq_tiles) == 0
        tail = ubuf[:, tile:tile + CONV_HALO, :]
        ubuf[:, 0:CONV_HALO, :] = jnp.where(seq_start, 0.0, tail)
        for c in range(n_slab):
            ubuf[c, CONV_HALO:CONV_HALO + tile, :] = glu[:, c * LANES:(c + 1) * LANES]

    @pl.when(i == 0)
    def _():
        mixer_part1()

    @pl.when(i == n_tiles)
    def _():
        mlp_block()

    @pl.when((i > 0) & (i < n_tiles))
    def _():
        mlp_block()
        mixer_part1()

    @pl.when(i < n_tiles)
    def _():
        base = CONV_HALO - (CONV_WIDTH - 1)

        def conv_rows(j, carry):
            r0 = pl.multiple_of(j * CONV_ROWS, CONV_ROWS)
            for c in range(n_slab):
                lanes = slice(c * LANES, (c + 1) * LANES)
                acc = jnp.broadcast_to(bdw_ref[mixer_idx:mixer_idx + 1, lanes], (CONV_ROWS, LANES))
                for k in range(CONV_WIDTH):
                    acc = acc + (wdw_ref[mixer_idx, k:k + 1, lanes]
                                 * ubuf[c, pl.ds(r0 + (base + k), CONV_ROWS), :])
                cbuf[pl.ds(r0, CONV_ROWS), lanes] = acc
            return carry

        lax.fori_loop(0, tile // CONV_ROWS, conv_rows, 0)

        v = _layernorm(cbuf[...], _row(lng_ref, mixer_idx), _row(lnb_ref, mixer_idx))
        v = v * jax.nn.sigmoid(v)
        mid[...] = x_ref[0] + _dot(v.astype(_BF16), w2s[...]) + _row(b2_ref, mixer_idx)


def _pool_layer_kernel(x_ref, g_ref, w_ref, b_ref, sc_ref, gm_ref, gf_ref, m1_hbm, m2_hbm,
                       o_ref, mid, pbuf, pws, m1s, m2s, stage, sem,
                       *, tile, d, seq_tiles, n_tiles, final_norm, mixer_idx, layer):
    del n_tiles
    i = pl.program_id(0)
    n_slab = d // LANES
    gw = d // len(POOL_WINDOWS)
    f = m2s.shape[0]

    @pl.when(i == 0)
    def _():
        pbuf[:, tile:tile + POOL_HALO, :] = jnp.zeros((n_slab, POOL_HALO, LANES), _F32)
        pws[...] = w_ref[mixer_idx].astype(_BF16)
        _load_weights(_weight_chunks(m1_hbm.at[layer], m1s, d, f)
                      + _weight_chunks(m2_hbm.at[layer], m2s, f, d), stage, sem)

    def step(with_mlp):
        x = x_ref[0]
        hn = _rmsnorm(x, _row(g_ref, layer))
        seq_start = (i % seq_tiles) == 0
        tail = pbuf[:, tile:tile + POOL_HALO, :]
        pbuf[:, 0:POOL_HALO, :] = jnp.where(seq_start, 0.0, tail)
        for c in range(n_slab):
            pbuf[c, POOL_HALO:POOL_HALO + tile, :] = hn[:, c * LANES:(c + 1) * LANES]

        pos = (i % seq_tiles) * tile + lax.broadcasted_iota(jnp.int32, (tile, 1), 0)
        mid_next = [None] * len(POOL_WINDOWS)
        bias = _row(b_ref, mixer_idx)
        scale = _row(sc_ref, mixer_idx)

        def pool_steps():
            for gi, win in enumerate(POOL_WINDOWS):
                cols = slice(gi * gw, (gi + 1) * gw)
                inv_cnt = 1.0 / jnp.minimum(pos + 1, win).astype(_F32)
                parts = []
                for c in range(gi * gw // LANES, (gi + 1) * gw // LANES):
                    tot = pbuf[c, POOL_HALO:POOL_HALO + tile, :]
                    for j in range(1, win):
                        tot = tot + pbuf[c, POOL_HALO - j:POOL_HALO - j + tile, :]
                    parts.append(tot)
                p = jnp.concatenate(parts, axis=-1) * inv_cnt - hn[:, cols]
                y = _dot(p.astype(_BF16), pws[gi]) + bias[:, cols]
                mid_next[gi] = x[:, cols] + y * scale[:, cols]
                yield

        gens = [pool_steps()]
        if with_mlp:
            gens.insert(0, _mlp_steps(mid, _row(gm_ref, layer), m1s, m2s, gf_ref[...], o_ref,
                                      final_norm))
        _interleave(*gens)
        for gi in range(len(POOL_WINDOWS)):
            mid[:, gi * gw:(gi + 1) * gw] = mid_next[gi]

    @pl.when(i == 0)
    def _():
        step(False)

    @pl.when(i > 0)
    def _():
        step(True)


def _const_spec(shape):
    return pl.BlockSpec(shape, lambda i: (0,) * len(shape), pipeline_mode=pl.Buffered(1))


def _layer_call(body, x, params, hbm_weights, scratch, *, tile, final_norm, mixer_idx, layer,
                name):
    bsz, seq, d = x.shape
    seq_tiles = seq // tile
    n_tiles = bsz * seq_tiles

    def in_map(i):
        j = jnp.minimum(i, n_tiles - 1)
        return (j // seq_tiles, j % seq_tiles, 0)

    def out_map(i):
        j = jnp.maximum(i - 1, 0)
        return (j // seq_tiles, j % seq_tiles, 0)

    return pl.pallas_call(
        functools.partial(body, tile=tile, d=d, seq_tiles=seq_tiles, n_tiles=n_tiles,
                          final_norm=final_norm, mixer_idx=mixer_idx, layer=layer),
        grid=(n_tiles + 1,),
        in_specs=([pl.BlockSpec((1, tile, d), in_map)] + [_const_spec(p.shape) for p in params]
                  + [pl.BlockSpec(memory_space=pl.ANY)] * len(hbm_weights)),
        out_specs=pl.BlockSpec((1, tile, d), out_map),
        out_shape=jax.ShapeDtypeStruct(x.shape, x.dtype),
        scratch_shapes=[pltpu.VMEM((tile, d), _F32)] + scratch + [
            pltpu.VMEM((2, STAGE_ROWS, STAGE_COLS), _F32), pltpu.SemaphoreType.DMA((2,))],
        compiler_params=pltpu.CompilerParams(
            dimension_semantics=("arbitrary",), vmem_limit_bytes=VMEM_LIMIT),
        name=name,
    )(x, *params, *hbm_weights)


def kernel(x, norm_mix, norm_mlp, conv_w1, conv_b1, conv_dw, conv_dwb, conv_ln_g, conv_ln_b,
           conv_w2, conv_b2, pool_w, pool_b, pool_scale, mlp_w1, mlp_w2, final_norm):
    depth, d = norm_mix.shape
    f = mlp_w1.shape[-1]

    gf = final_norm.reshape(1, d)
    tile = TILE
    pool_bias = pool_b.reshape(pool_b.shape[0], d)
    mlp_scratch = [pltpu.VMEM((f // MLP_F_CHUNK, d, MLP_F_CHUNK), _BF16), pltpu.VMEM((f, d), _BF16)]
    h = x
    ic = ip = 0
    for i in range(depth):
        last = i == depth - 1
        if i % 2 == 0:
            params = [norm_mix, conv_b1, conv_dw, conv_dwb, conv_ln_g, conv_ln_b, conv_b2,
                      norm_mlp, gf]
            scratch = [pltpu.VMEM((d // LANES, CONV_HALO + tile, LANES), _F32),
                       pltpu.VMEM((tile, d), _F32),
                       pltpu.VMEM((2, d, d), _BF16), pltpu.VMEM((d, d), _BF16)] + mlp_scratch
            h = _layer_call(_conv_layer_kernel, h, params,
                            [conv_w1, conv_w2, mlp_w1, mlp_w2], scratch, tile=tile,
                            final_norm=last, mixer_idx=ic, layer=i, name="conv_layer")
            ic += 1
        else:
            params = [norm_mix, pool_w, pool_bias, pool_scale, norm_mlp, gf]
            scratch = [pltpu.VMEM((d // LANES, POOL_HALO + tile, LANES), _F32),
                       pltpu.VMEM(pool_w.shape[1:], _BF16)] + mlp_scratch
            h = _layer_call(_pool_layer_kernel, h, params, [mlp_w1, mlp_w2], scratch, tile=tile,
                            final_norm=last, mixer_idx=ip, layer=i, name="pool_layer")
            ip += 1
    return h
```

```python
import functools

import jax
import jax.numpy as jnp
from jax import lax
from jax.experimental import pallas as pl
from jax.experimental.pallas import tpu as pltpu

EPS = 1e-6
CONV_WIDTH = 31
POOL_WINDOWS = (2, 4, 8, 16)

LANES = 128
CONV_HALO = 32
POOL_HALO = 16
CONV_ROWS = 128
MLP_F_CHUNK = 1024
TILE = 512
STAGE_ROWS = 512
STAGE_COLS = 1024
VMEM_LIMIT = 60 * 1024 * 1024

_F32 = jnp.float32
_BF16 = jnp.bfloat16


def _rmsnorm(x, g):
    ms = jnp.mean(x * x, axis=-1, keepdims=True)
    return x * lax.rsqrt(ms + EPS) * g


def _layernorm(x, g, b):
    mu = jnp.mean(x, axis=-1, keepdims=True)
    xc = x - mu
    var = jnp.mean(xc * xc, axis=-1, keepdims=True)
    return xc * lax.rsqrt(var + EPS) * g + b


def _dot(a, b):
    return jnp.dot(a, b, preferred_element_type=_F32)


def _row(ref, idx):
    return ref[idx:idx + 1, :]


def _interleave(*gens):
    gens = list(gens)
    while gens:
        for g in list(gens):
            try:
                next(g)
            except StopIteration:
                gens.remove(g)


def _weight_chunks(src, dst, n_rows, n_cols):
    out = []
    for c in range(n_cols // STAGE_COLS):
        for r in range(0, n_rows, STAGE_ROWS):
            s = src.at[pl.ds(r, STAGE_ROWS), pl.ds(c * STAGE_COLS, STAGE_COLS)]
            if n_cols == STAGE_COLS:
                d = dst.at[pl.ds(r, STAGE_ROWS), :]
            else:
                d = dst.at[c, pl.ds(r, STAGE_ROWS), :]
            out.append((s, d))
    return out


def _load_weights(chunks, stage, sem):
    def copy(k):
        return pltpu.make_async_copy(chunks[k][0], stage.at[k % 2], sem.at[k % 2])

    copy(0).start()
    for k in range(len(chunks)):
        if k + 1 < len(chunks):
            copy(k + 1).start()
        copy(k).wait()
        chunks[k][1][...] = stage[k % 2].astype(_BF16)


def _mlp_steps(mid, g, m1s, m2s, gf, o_ref, final_norm):
    h = mid[...]
    hb = _rmsnorm(h, g).astype(_BF16)
    hidden = []
    for c in range(m1s.shape[0]):
        u = jnp.maximum(_dot(hb, m1s[c]), 0.0)
        hidden.append((u * u).astype(_BF16))
        yield
    acc = h + _dot(jnp.concatenate(hidden, axis=1), m2s[...])
    if final_norm:
        acc = _rmsnorm(acc, gf)
    o_ref[0] = acc


def _conv_layer_kernel(x_ref, g_ref, b1_ref, wdw_ref, bdw_ref, lng_ref, lnb_ref, b2_ref,
                       gm_ref, gf_ref, w1_hbm, w2_hbm, m1_hbm, m2_hbm, o_ref,
                       mid, ubuf, cbuf, w1s, w2s, m1s, m2s, stage, sem,
                       *, tile, d, seq_tiles, n_tiles, final_norm, mixer_idx, layer):
    i = pl.program_id(0)
    n_slab = d // LANES
    f = m2s.shape[0]

    @pl.when(i == 0)
    def _():
        ubuf[:, tile:tile + CONV_HALO, :] = jnp.zeros((n_slab, CONV_HALO, LANES), _F32)
        _load_weights(_weight_chunks(w1_hbm.at[mixer_idx], w1s, d, 2 * d)
                      + _weight_chunks(w2_hbm.at[mixer_idx], w2s, d, d)
                      + _weight_chunks(m1_hbm.at[layer], m1s, d, f)
                      + _weight_chunks(m2_hbm.at[layer], m2s, f, d), stage, sem)

    def mlp_block():
        for _ in _mlp_steps(mid, _row(gm_ref, layer), m1s, m2s, gf_ref[...], o_ref, final_norm):
            pass

    def mixer_part1():
        x = x_ref[0]
        hn = _rmsnorm(x, _row(g_ref, layer)).astype(_BF16)
        b1 = _row(b1_ref, mixer_idx)
        glu = (_dot(hn, w1s[0]) + b1[:, :d]) * jax.nn.sigmoid(_dot(hn, w1s[1]) + b1[:, d:])
        seq_start = (i % seq_tiles) == 0
        tail = ubuf[:, tile:tile + CONV_HALO, :]
        ubuf[:, 0:CONV_HALO, :] = jnp.where(seq_start, 0.0, tail)
        for c in range(n_slab):
            ubuf[c, CONV_HALO:CONV_HALO + tile, :] = glu[:, c * LANES:(c + 1) * LANES]

    @pl.when(i == 0)
    def _():
        mixer_part1()

    @pl.when(i == n_tiles)
    def _():
        mlp_block()

    @pl.when((i > 0) & (i < n_tiles))
    def _():
        mlp_block()
        mixer_part1()

    @pl.when(i < n_tiles)
    def _():
        base = CONV_HALO - (CONV_WIDTH - 1)

        def conv_rows(j, carry):
            r0 = pl.multiple_of(j * CONV_ROWS, CONV_ROWS)
            for c in range(n_slab):
                lanes = slice(c * LANES, (c + 1) * LANES)
                acc = jnp.broadcast_to(bdw_ref[mixer_idx:mixer_idx + 1, lanes], (CONV_ROWS, LANES))
                for k in range(CONV_WIDTH):
                    acc = acc + (wdw_ref[mixer_idx, k:k + 1, lanes]
                                 * ubuf[c, pl.ds(r0 + (base + k), CONV_ROWS), :])
                cbuf[pl.ds(r0, CONV_ROWS), lanes] = acc
            return carry

        lax.fori_loop(0, tile // CONV_ROWS, conv_rows, 0)

        v = _layernorm(cbuf[...], _row(lng_ref, mixer_idx), _row(lnb_ref, mixer_idx))
        v = v * jax.nn.sigmoid(v)
        mid[...] = x_ref[0] + _dot(v.astype(_BF16), w2s[...]) + _row(b2_ref, mixer_idx)


def _pool_layer_kernel(x_ref, g_ref, w_ref, b_ref, sc_ref, gm_ref, gf_ref, m1_hbm, m2_hbm,
                       o_ref, mid, pbuf, pws, m1s, m2s, stage, sem,
                       *, tile, d, seq_tiles, n_tiles, final_norm, mixer_idx, layer):
    del n_tiles
    i = pl.program_id(0)
    n_slab = d // LANES
    gw = d // len(POOL_WINDOWS)
    f = m2s.shape[0]

    @pl.when(i == 0)
    def _():
        pbuf[:, tile:tile + POOL_HALO, :] = jnp.zeros((n_slab, POOL_HALO, LANES), _F32)
        pws[...] = w_ref[mixer_idx].astype(_BF16)
        _load_weights(_weight_chunks(m1_hbm.at[layer], m1s, d, f)
                      + _weight_chunks(m2_hbm.at[layer], m2s, f, d), stage, sem)

    def step(with_mlp):
        x = x_ref[0]
        hn = _rmsnorm(x, _row(g_ref, layer))
        seq_start = (i % seq_tiles) == 0
        tail = pbuf[:, tile:tile + POOL_HALO, :]
        pbuf[:, 0:POOL_HALO, :] = jnp.where(seq_start, 0.0, tail)
        for c in range(n_slab):
            pbuf[c, POOL_HALO:POOL_HALO + tile, :] = hn[:, c * LANES:(c + 1) * LANES]

        pos = (i % seq_tiles) * tile + lax.broadcasted_iota(jnp.int32, (tile, 1), 0)
        mid_next = [None] * len(POOL_WINDOWS)
        bias = _row(b_ref, mixer_idx)
        scale = _row(sc_ref, mixer_idx)

        def pool_steps():
            for gi, win in enumerate(POOL_WINDOWS):
                cols = slice(gi * gw, (gi + 1) * gw)
                inv_cnt = 1.0 / jnp.minimum(pos + 1, win).astype(_F32)
                parts = []
                for c in range(gi * gw // LANES, (gi + 1) * gw // LANES):
                    tot = pbuf[c, POOL_HALO:POOL_HALO + tile, :]
                    for j in range(1, win):
                        tot = tot + pbuf[c, POOL_HALO - j:POOL_HALO - j + tile, :]
                    parts.append(tot)
                p = jnp.concatenate(parts, axis=-1) * inv_cnt - hn[:, cols]
                y = _dot(p.astype(_BF16), pws[gi]) + bias[:, cols]
                mid_next[gi] = x[:, cols] + y * scale[:, cols]
                yield

        gens = [pool_steps()]
        if with_mlp:
            gens.insert(0, _mlp_steps(mid, _row(gm_ref, layer), m1s, m2s, gf_ref[...], o_ref,
                                      final_norm))
        _interleave(*gens)
        for gi in range(len(POOL_WINDOWS)):
            mid[:, gi * gw:(gi + 1) * gw] = mid_next[gi]

    @pl.when(i == 0)
    def _():
        step(False)

    @pl.when(i > 0)
    def _():
        step(True)


def _const_spec(shape):
    return pl.BlockSpec(shape, lambda i: (0,) * len(shape), pipeline_mode=pl.Buffered(1))


def _layer_call(body, x, params, hbm_weights, scratch, *, tile, final_norm, mixer_idx, layer,
                name):
    bsz, seq, d = x.shape
    seq_tiles = seq // tile
    n_tiles = bsz * seq_tiles

    def in_map(i):
        j = jnp.minimum(i, n_tiles - 1)
        return (j // seq_tiles, j % seq_tiles, 0)

    def out_map(i):
        j = jnp.maximum(i - 1, 0)
        return (j // seq_tiles, j % seq_tiles, 0)

    return pl.pallas_call(
        functools.partial(body, tile=tile, d=d, seq_tiles=seq_tiles, n_tiles=n_tiles,
                          final_norm=final_norm, mixer_idx=mixer_idx, layer=layer),
        grid=(n_tiles + 1,),
        in_specs=([pl.BlockSpec((1, tile, d), in_map)] + [_const_spec(p.shape) for p in params]
                  + [pl.BlockSpec(memory_space=pl.ANY)] * len(hbm_weights)),
        out_specs=pl.BlockSpec((1, tile, d), out_map),
        out_shape=jax.ShapeDtypeStruct(x.shape, x.dtype),
        scratch_shapes=[pltpu.VMEM((tile, d), _F32)] + scratch + [
            pltpu.VMEM((2, STAGE_ROWS, STAGE_COLS), _F32), pltpu.SemaphoreType.DMA((2,))],
        compiler_params=pltpu.CompilerParams(
            dimension_semantics=("arbitrary",), vmem_limit_bytes=VMEM_LIMIT),
        name=name,
    )(x, *params, *hbm_weights)


def kernel(x, norm_mix, norm_mlp, conv_w1, conv_b1, conv_dw, conv_dwb, conv_ln_g, conv_ln_b,
           conv_w2, conv_b2, pool_w, pool_b, pool_scale, mlp_w1, mlp_w2, final_norm):
    depth, d = norm_mix.shape
    f = mlp_w1.shape[-1]

    gf = final_norm.reshape(1, d)
    tile = TILE
    pool_bias = pool_b.reshape(pool_b.shape[0], d)
    mlp_scratch = [pltpu.VMEM((f // MLP_F_CHUNK, d, MLP_F_CHUNK), _BF16), pltpu.VMEM((f, d), _BF16)]
    h = x
    ic = ip = 0
    for i in range(depth):
        last = i == depth - 1
        if i % 2 == 0:
            params = [norm_mix, conv_b1, conv_dw, conv_dwb, conv_ln_g, conv_ln_b, conv_b2,
                      norm_mlp, gf]
            scratch = [pltpu.VMEM((d // LANES, CONV_HALO + tile, LANES), _F32),
                       pltpu.VMEM((tile, d), _F32),
                       pltpu.VMEM((2, d, d), _BF16), pltpu.VMEM((d, d), _BF16)] + mlp_scratch
            h = _layer_call(_conv_layer_kernel, h, params,
                            [conv_w1, conv_w2, mlp_w1, mlp_w2], scratch, tile=tile,
                            final_norm=last, mixer_idx=ic, layer=i, name="conv_layer")
            ic += 1
        else:
            params = [norm_mix, pool_w, pool_bias, pool_scale, norm_mlp, gf]
            scratch = [pltpu.VMEM((d // LANES, POOL_HALO + tile, LANES), _F32),
                       pltpu.VMEM(pool_w.shape[1:], _BF16)] + mlp_scratch
            h = _layer_call(_pool_layer_kernel, h, params, [mlp_w1, mlp_w2], scratch, tile=tile,
                            final_norm=last, mixer_idx=ip, layer=i, name="pool_layer")
            ip += 1
    return h
```

```python
import functools

import jax
import jax.numpy as jnp
from jax import lax
from jax.experimental import pallas as pl
from jax.experimental.pallas import tpu as pltpu

EPS = 1e-6
CONV_WIDTH = 31
POOL_WINDOWS = (2, 4, 8, 16)

LANES = 128
CONV_HALO = 32
POOL_HALO = 16
CONV_ROWS = 64
MLP_F_CHUNK = 1024
TILE = 512
STAGE_ROWS = 512
STAGE_COLS = 1024
STAGE_SLOTS = 4
VMEM_LIMIT = 60 * 1024 * 1024

_F32 = jnp.float32
_BF16 = jnp.bfloat16


def _rmsnorm(x, g):
    ms = jnp.mean(x * x, axis=-1, keepdims=True)
    return x * lax.rsqrt(ms + EPS) * g


def _layernorm(x, g, b):
    mu = jnp.mean(x, axis=-1, keepdims=True)
    xc = x - mu
    var = jnp.mean(xc * xc, axis=-1, keepdims=True)
    return xc * lax.rsqrt(var + EPS) * g + b


def _dot(a, b):
    return jnp.dot(a, b, preferred_element_type=_F32)


def _row(ref, idx):
    return ref[idx:idx + 1, :]


def _interleave(*gens):
    gens = list(gens)
    while gens:
        for g in list(gens):
            try:
                next(g)
            except StopIteration:
                gens.remove(g)


def _weight_chunks(src, dst, n_rows, n_cols):
    out = []
    for c in range(n_cols // STAGE_COLS):
        for r in range(0, n_rows, STAGE_ROWS):
            s = src.at[pl.ds(r, STAGE_ROWS), pl.ds(c * STAGE_COLS, STAGE_COLS)]
            if n_cols == STAGE_COLS:
                d = dst.at[pl.ds(r, STAGE_ROWS), :]
            else:
                d = dst.at[c, pl.ds(r, STAGE_ROWS), :]
            out.append((s, d))
    return out


def _load_weights(chunks, stage, sem):
    n_slots = stage.shape[0]

    def copy(k):
        return pltpu.make_async_copy(chunks[k][0], stage.at[k % n_slots], sem.at[k % n_slots])

    for k in range(min(n_slots - 1, len(chunks))):
        copy(k).start()
    for k in range(len(chunks)):
        if k + n_slots - 1 < len(chunks):
            copy(k + n_slots - 1).start()
        copy(k).wait()
        chunks[k][1][...] = stage[k % n_slots].astype(_BF16)


def _mlp_steps(mid, g, m1s, m2s, gf, o_ref, final_norm):
    h = mid[...]
    hb = _rmsnorm(h, g).astype(_BF16)
    hidden = []
    for c in range(m1s.shape[0]):
        u = jnp.maximum(_dot(hb, m1s[c]), 0.0)
        hidden.append((u * u).astype(_BF16))
        yield
    acc = h + _dot(jnp.concatenate(hidden, axis=1), m2s[...])
    if final_norm:
        acc = _rmsnorm(acc, gf)
    o_ref[0] = acc


def _conv_layer_kernel(x_ref, g_ref, b1_ref, wdw_ref, bdw_ref, lng_ref, lnb_ref, b2_ref,
                       gm_ref, gf_ref, w1_hbm, w2_hbm, m1_hbm, m2_hbm, o_ref,
                       mid, ubuf, cbuf, w1s, w2s, m1s, m2s, stage, sem,
                       *, tile, d, seq_tiles, n_tiles, final_norm, mixer_idx, layer):
    i = pl.program_id(0)
    n_slab = d // LANES
    f = m2s.shape[0]

    @pl.when(i == 0)
    def _():
        ubuf[:, tile:tile + CONV_HALO, :] = jnp.zeros((n_slab, CONV_HALO, LANES), _F32)
        _load_weights(_weight_chunks(w1_hbm.at[mixer_idx], w1s, d, 2 * d)
                      + _weight_chunks(w2_hbm.at[mixer_idx], w2s, d, d)
                      + _weight_chunks(m1_hbm.at[layer], m1s, d, f)
                      + _weight_chunks(m2_hbm.at[layer], m2s, f, d), stage, sem)

    def mlp_block():
        for _ in _mlp_steps(mid, _row(gm_ref, layer), m1s, m2s, gf_ref[...], o_ref, final_norm):
            pass

    def mixer_part1():
        x = x_ref[0]
        hn = _rmsnorm(x, _row(g_ref, layer)).astype(_BF16)
        b1 = _row(b1_ref, mixer_idx)
        glu = (_dot(hn, w1s[0]) + b1[:, :d]) * jax.nn.sigmoid(_dot(hn, w1s[1]) + b1[:, d:])
        seq_start = (i % seq_tiles) == 0
        tail = ubuf[:, tile:tile + CONV_HALO, :]
        ubuf[:, 0:CONV_HALO, :] = jnp.where(seq_start, 0.0, tail)
        for c in range(n_slab):
            ubuf[c, CONV_HALO:CONV_HALO + tile, :] = glu[:, c * LANES:(c + 1) * LANES]

    @pl.when(i == 0)
    def _():
        mixer_part1()

    @pl.when(i == n_tiles)
    def _():
        mlp_block()

    @pl.when((i > 0) & (i < n_tiles))
    def _():
        mlp_block()
        mixer_part1()

    @pl.when(i < n_tiles)
    def _():
        base = CONV_HALO - (CONV_WIDTH - 1)

        def conv_rows(j, carry):
            r0 = pl.multiple_of(j * CONV_ROWS, CONV_ROWS)
            for c in range(n_slab):
                lanes = slice(c * LANES, (c + 1) * LANES)
                acc = jnp.broadcast_to(bdw_ref[mixer_idx:mixer_idx + 1, lanes], (CONV_ROWS, LANES))
                for k in range(CONV_WIDTH):
                    acc = acc + (wdw_ref[mixer_idx, k:k + 1, lanes]
                                 * ubuf[c, pl.ds(r0 + (base + k), CONV_ROWS), :])
                cbuf[pl.ds(r0, CONV_ROWS), lanes] = acc
            return carry

        lax.fori_loop(0, tile // CONV_ROWS, conv_rows, 0)

        v = _layernorm(cbuf[...], _row(lng_ref, mixer_idx), _row(lnb_ref, mixer_idx))
        v = v * jax.nn.sigmoid(v)
        mid[...] = x_ref[0] + _dot(v.astype(_BF16), w2s[...]) + _row(b2_ref, mixer_idx)


def _pool_layer_kernel(x_ref, g_ref, w_ref, b_ref, sc_ref, gm_ref, gf_ref, m1_hbm, m2_hbm,
                       o_ref, mid, pbuf, pws, m1s, m2s, stage, sem,
                       *, tile, d, seq_tiles, n_tiles, final_norm, mixer_idx, layer):
    del n_tiles
    i = pl.program_id(0)
    n_slab = d // LANES
    gw = d // len(POOL_WINDOWS)
    f = m2s.shape[0]

    @pl.when(i == 0)
    def _():
        pbuf[:, tile:tile + POOL_HALO, :] = jnp.zeros((n_slab, POOL_HALO, LANES), _F32)
        pws[...] = w_ref[mixer_idx].astype(_BF16)
        _load_weights(_weight_chunks(m1_hbm.at[layer], m1s, d, f)
                      + _weight_chunks(m2_hbm.at[layer], m2s, f, d), stage, sem)

    def step(with_mlp):
        x = x_ref[0]
        hn = _rmsnorm(x, _row(g_ref, layer))
        seq_start = (i % seq_tiles) == 0
        tail = pbuf[:, tile:tile + POOL_HALO, :]
        pbuf[:, 0:POOL_HALO, :] = jnp.where(seq_start, 0.0, tail)
        for c in range(n_slab):
            pbuf[c, POOL_HALO:POOL_HALO + tile, :] = hn[:, c * LANES:(c + 1) * LANES]

        pos = (i % seq_tiles) * tile + lax.broadcasted_iota(jnp.int32, (tile, 1), 0)
        mid_next = [None] * len(POOL_WINDOWS)
        bias = _row(b_ref, mixer_idx)
        scale = _row(sc_ref, mixer_idx)

        def pool_steps():
            for gi, win in enumerate(POOL_WINDOWS):
                cols = slice(gi * gw, (gi + 1) * gw)
                inv_cnt = 1.0 / jnp.minimum(pos + 1, win).astype(_F32)
                parts = []
                for c in range(gi * gw // LANES, (gi + 1) * gw // LANES):
                    tot = pbuf[c, POOL_HALO:POOL_HALO + tile, :]
                    for j in range(1, win):
                        tot = tot + pbuf[c, POOL_HALO - j:POOL_HALO - j + tile, :]
                    parts.append(tot)
                p = jnp.concatenate(parts, axis=-1) * inv_cnt - hn[:, cols]
                y = _dot(p.astype(_BF16), pws[gi]) + bias[:, cols]
                mid_next[gi] = x[:, cols] + y * scale[:, cols]
                yield

        gens = [pool_steps()]
        if with_mlp:
            gens.insert(0, _mlp_steps(mid, _row(gm_ref, layer), m1s, m2s, gf_ref[...], o_ref,
                                      final_norm))
        _interleave(*gens)
        for gi in range(len(POOL_WINDOWS)):
            mid[:, gi * gw:(gi + 1) * gw] = mid_next[gi]

    @pl.when(i == 0)
    def _():
        step(False)

    @pl.when(i > 0)
    def _():
        step(True)


def _const_spec(shape):
    return pl.BlockSpec(shape, lambda i: (0,) * len(shape), pipeline_mode=pl.Buffered(1))


def _layer_call(body, x, params, hbm_weights, scratch, *, tile, final_norm, mixer_idx, layer,
                name):
    bsz, seq, d = x.shape
    seq_tiles = seq // tile
    n_tiles = bsz * seq_tiles

    def in_map(i):
        j = jnp.minimum(i, n_tiles - 1)
        return (j // seq_tiles, j % seq_tiles, 0)

    def out_map(i):
        j = jnp.maximum(i - 1, 0)
        return (j // seq_tiles, j % seq_tiles, 0)

    return pl.pallas_call(
        functools.partial(body, tile=tile, d=d, seq_tiles=seq_tiles, n_tiles=n_tiles,
                          final_norm=final_norm, mixer_idx=mixer_idx, layer=layer),
        grid=(n_tiles + 1,),
        in_specs=([pl.BlockSpec((1, tile, d), in_map)] + [_const_spec(p.shape) for p in params]
                  + [pl.BlockSpec(memory_space=pl.ANY)] * len(hbm_weights)),
        out_specs=pl.BlockSpec((1, tile, d), out_map),
        out_shape=jax.ShapeDtypeStruct(x.shape, x.dtype),
        scratch_shapes=[pltpu.VMEM((tile, d), _F32)] + scratch + [
            pltpu.VMEM((STAGE_SLOTS, STAGE_ROWS, STAGE_COLS), _F32),
            pltpu.SemaphoreType.DMA((STAGE_SLOTS,))],
        compiler_params=pltpu.CompilerParams(
            dimension_semantics=("arbitrary",), vmem_limit_bytes=VMEM_LIMIT),
        name=name,
    )(x, *params, *hbm_weights)


def kernel(x, norm_mix, norm_mlp, conv_w1, conv_b1, conv_dw, conv_dwb, conv_ln_g, conv_ln_b,
           conv_w2, conv_b2, pool_w, pool_b, pool_scale, mlp_w1, mlp_w2, final_norm):
    depth, d = norm_mix.shape
    f = mlp_w1.shape[-1]

    gf = final_norm.reshape(1, d)
    tile = TILE
    pool_bias = pool_b.reshape(pool_b.shape[0], d)
    mlp_scratch = [pltpu.VMEM((f // MLP_F_CHUNK, d, MLP_F_CHUNK), _BF16), pltpu.VMEM((f, d), _BF16)]
    h = x
    ic = ip = 0
    for i in range(depth):
        last = i == depth - 1
        if i % 2 == 0:
            params = [norm_mix, conv_b1, conv_dw, conv_dwb, conv_ln_g, conv_ln_b, conv_b2,
                      norm_mlp, gf]
            scratch = [pltpu.VMEM((d // LANES, CONV_HALO + tile, LANES), _F32),
                       pltpu.VMEM((tile, d), _F32),
                       pltpu.VMEM((2, d, d), _BF16), pltpu.VMEM((d, d), _BF16)] + mlp_scratch
            h = _layer_call(_conv_layer_kernel, h, params,
                            [conv_w1, conv_w2, mlp_w1, mlp_w2], scratch, tile=tile,
                            final_norm=last, mixer_idx=ic, layer=i, name="conv_layer")
            ic += 1
        else:
            params = [norm_mix, pool_w, pool_bias, pool_scale, norm_mlp, gf]
            scratch = [pltpu.VMEM((d // LANES, POOL_HALO + tile, LANES), _F32),
                       pltpu.VMEM(pool_w.shape[1:], _BF16)] + mlp_scratch
            h = _layer_call(_pool_layer_kernel, h, params, [mlp_w1, mlp_w2], scratch, tile=tile,
                            final_norm=last, mixer_idx=ip, layer=i, name="pool_layer")
            ip += 1
    return h
```
